```python
import math
import jax
import jax.numpy as jnp
from jax import lax
import numpy as np

D_MODEL = 1024
BATCH = 8
SEQ = 4096
DEPTH = 2

D_SSM = D_MODEL // 2
SSM_GROUP = 16
N_SSM_GROUPS = D_SSM // SSM_GROUP
SSM_STATE = 64
HEAD_DIM = 64
D_ATTN = D_MODEL - D_SSM
N_HEADS = D_ATTN // HEAD_DIM
N_KV_HEADS = 2
GQA = N_HEADS // N_KV_HEADS
D_KV = N_KV_HEADS * HEAD_DIM
D_IN = D_SSM + D_ATTN + 6 * D_KV + 3 * N_HEADS
CMP_BLOCK = 32
CMP_STRIDE = 16
SEL_BLOCK = 64
SEL_TOPK = 16
N_LOCAL_BLOCKS = 2
WINDOW = 512
Q_BLOCK = 128
N_BUCKETS = 32
MAX_DISTANCE = 128
N_GROUPS = 4
EXPERTS_PER_GROUP = 8
TOPK_EXPERT = 2
D_EXPERT = D_MODEL // 4
EPS = 1e-6
NEG = -1e30
BIG = 1e9

kernel_name = 'hymba_s5_nsa_hier_moe_block'


def _rmsnorm(x, g):
    x32 = x.astype(jnp.float32)
    y = x32 * lax.rsqrt(jnp.mean(x32 * x32, axis=-1, keepdims=True) + EPS)
    return (y * g.astype(jnp.float32)).astype(x.dtype)


def _t5_bucket(dist):
    n = jnp.maximum(dist, 0)
    max_exact = N_BUCKETS // 2
    nf = jnp.maximum(n, 1).astype(jnp.float32)
    large = max_exact + (jnp.log(nf / max_exact) / math.log(MAX_DISTANCE / max_exact)
                         * (N_BUCKETS - max_exact)).astype(jnp.int32)
    large = jnp.minimum(large, N_BUCKETS - 1)
    return jnp.where(n < max_exact, n, large)


def _masked_softmax(s, valid):
    p = jax.nn.softmax(jnp.where(valid, s, NEG), axis=-1)
    return jnp.where(valid, p, 0.0)


def _s5_mixer(u, a_re, a_im, b_re, b_im, c_re, c_im, d, log_dt, w_glu):
    bsz, seq, _ = u.shape
    f32 = jnp.float32
    u32 = u.astype(f32).reshape(bsz, seq, N_SSM_GROUPS, SSM_GROUP)
    lam = lax.complex(a_re.astype(f32), a_im.astype(f32))
    dt = jnp.exp(log_dt.astype(f32))[:, None]
    lam_bar = jnp.exp(lam * dt)
    b = lax.complex(b_re.astype(f32), b_im.astype(f32))
    b_bar = ((lam_bar - 1.0) / lam)[..., None] * b
    c = lax.complex(c_re.astype(f32), c_im.astype(f32))
    bu = jnp.einsum('blgh,gph->blgp', u32, b_bar)
    a = jnp.broadcast_to(lam_bar, bu.shape)

    def combine(left, right):
        a1, b1 = left
        a2, b2 = right
        return a1 * a2, a2 * b1 + b2

    _, states = lax.associative_scan(combine, (a, bu), axis=1)
    y = jnp.real(jnp.einsum('blgp,ghp->blgh', states, c)) + d.astype(f32) * u32
    y = jax.nn.gelu(y.reshape(bsz, seq, D_SSM)).astype(u.dtype)
    ya, yb = jnp.split(y @ w_glu, 2, axis=-1)
    return ya * jax.nn.sigmoid(yb)


def _nsa_mixer(q, k_cmp, v_cmp, k_sel, v_sel, k_win, v_win, gates,
               cmp_pos, cmp_w1, cmp_w2, rel_bias):
    f32 = jnp.float32
    bsz, seq = q.shape[:2]
    dtype = q.dtype
    scale = HEAD_DIM ** -0.5
    q = q.reshape(bsz, seq, N_KV_HEADS, GQA, HEAD_DIM)
    kvs = [a.reshape(bsz, seq, N_KV_HEADS, HEAD_DIM) for a in (k_cmp, v_cmp, k_sel, v_sel, k_win, v_win)]
    k_cmp, v_cmp, k_sel, v_sel, k_win, v_win = kvs
    bias_tab = rel_bias.astype(f32).reshape(N_BUCKETS, N_KV_HEADS, GQA)
    t = jnp.arange(seq)

    n_cmp = (seq - CMP_BLOCK) // CMP_STRIDE + 1
    cidx = np.arange(n_cmp)[:, None] * CMP_STRIDE + np.arange(CMP_BLOCK)[None, :]

    def compress(kv, i):
        blocks = kv[:, cidx] + cmp_pos[i][None, None, :, None, :]
        hid = jax.nn.gelu(jnp.einsum('bnlhd,lde->bnhe', blocks, cmp_w1[i]))
        return jnp.einsum('bnhe,ef->bnhf', hid, cmp_w2[i])

    kc = compress(k_cmp, 0)
    vc = compress(v_cmp, 1)
    cmp_end = jnp.asarray(cidx[:, -1])
    dist_c = t[:, None] - cmp_end[None, :]
    valid_c = dist_c >= 0
    bias_c = bias_tab[_t5_bucket(dist_c)].transpose(2, 3, 0, 1)
    s_c = jnp.einsum('bthgd,bnhd->bhgtn', q, kc).astype(f32) * scale + bias_c
    p_c = _masked_softmax(s_c, valid_c)
    o_cmp = jnp.einsum('bhgtn,bnhd->bthgd', p_c.astype(dtype), vc)

    n_sel = seq // SEL_BLOCK
    cs = np.arange(n_cmp)[:, None] * CMP_STRIDE
    ss = np.arange(n_sel)[None, :] * SEL_BLOCK
    overlap = np.clip(np.minimum(cs + CMP_BLOCK, ss + SEL_BLOCK) - np.maximum(cs, ss), 0, None) / CMP_BLOCK
    overlap = jnp.asarray(overlap, dtype=f32)
    imp = jnp.einsum('bhgtn,nj->bhtj', p_c, overlap)
    j = jnp.arange(n_sel)[None, :]
    blk_t = (t // SEL_BLOCK)[:, None]
    forced = (j == 0) | ((j <= blk_t) & (j > blk_t - N_LOCAL_BLOCKS))
    future = j > blk_t
    imp = jnp.where(forced, BIG, jnp.where(future, -BIG, imp))
    n_top = min(SEL_TOPK, n_sel)
    _, sel_idx = lax.top_k(imp, n_top)

    k_blk = k_sel.reshape(bsz, n_sel, SEL_BLOCK, N_KV_HEADS, HEAD_DIM).transpose(0, 3, 1, 2, 4)
    v_blk = v_sel.reshape(bsz, n_sel, SEL_BLOCK, N_KV_HEADS, HEAD_DIM).transpose(0, 3, 1, 2, 4)
    pad = ((0, 0), (WINDOW, 0), (0, 0), (0, 0))
    k_win_p = jnp.pad(k_win, pad)
    v_win_p = jnp.pad(v_win, pad)
    bi = jnp.arange(bsz)[:, None, None, None]
    hi = jnp.arange(N_KV_HEADS)[None, :, None, None]
    inner = jnp.arange(SEL_BLOCK)
    win_off = jnp.arange(Q_BLOCK + WINDOW) - WINDOW
    n_sk = n_top * SEL_BLOCK

    def query_block(c):
        t0 = c * Q_BLOCK
        tq = t0 + jnp.arange(Q_BLOCK)
        qc = lax.dynamic_slice_in_dim(q, t0, Q_BLOCK, axis=1)
        ic = lax.dynamic_slice_in_dim(sel_idx, t0, Q_BLOCK, axis=2)
        ks = k_blk[bi, hi, ic].reshape(bsz, N_KV_HEADS, Q_BLOCK, n_sk, HEAD_DIM)
        vs = v_blk[bi, hi, ic].reshape(bsz, N_KV_HEADS, Q_BLOCK, n_sk, HEAD_DIM)
        pos_s = (ic[..., None] * SEL_BLOCK + inner).reshape(bsz, N_KV_HEADS, Q_BLOCK, n_sk)
        dist_s = tq[None, None, :, None] - pos_s
        bias_s = jnp.moveaxis(bias_tab[_t5_bucket(dist_s), hi], -1, 2)
        s_s = jnp.einsum('bqhgd,bhqsd->bhgqs', qc, ks).astype(f32) * scale + bias_s
        p_s = _masked_softmax(s_s, (dist_s >= 0)[:, :, None])
        o_s = jnp.einsum('bhgqs,bhqsd->bqhgd', p_s.astype(dtype), vs)
        kw = lax.dynamic_slice_in_dim(k_win_p, t0, Q_BLOCK + WINDOW, axis=1)
        vw = lax.dynamic_slice_in_dim(v_win_p, t0, Q_BLOCK + WINDOW, axis=1)
        pos_w = t0 + win_off
        dist_w = tq[:, None] - pos_w[None, :]
        valid_w = (dist_w >= 0) & (dist_w < WINDOW) & (pos_w[None, :] >= 0)
        bias_w = bias_tab[_t5_bucket(dist_w)].transpose(2, 3, 0, 1)
        s_w = jnp.einsum('bqhgd,bkhd->bhgqk', qc, kw).astype(f32) * scale + bias_w
        p_w = _masked_softmax(s_w, valid_w)
        o_w = jnp.einsum('bhgqk,bkhd->bqhgd', p_w.astype(dtype), vw)
        return o_s, o_w

    o_sel, o_win = lax.map(query_block, jnp.arange(seq // Q_BLOCK))

    def unblock(o):
        return jnp.moveaxis(o, 0, 1).reshape(bsz, seq, N_KV_HEADS, GQA, HEAD_DIM)

    g = jax.nn.sigmoid(gates.astype(f32)).reshape(bsz, seq, N_KV_HEADS, GQA, 3).astype(dtype)
    o = g[..., 0:1] * o_cmp + g[..., 1:2] * unblock(o_sel) + g[..., 2:3] * unblock(o_win)
    return o.reshape(bsz, seq, D_ATTN)


def _hier_moe(h, rg_w, rg_b, re_w, re_b, w_gate, w_up, w_down):
    f32 = jnp.float32
    bsz, seq, d = h.shape
    ht = h.reshape(-1, d)
    pg = jax.nn.softmax((ht @ rg_w + rg_b).astype(f32), axis=-1)
    g_val, g_idx = lax.top_k(pg, 1)
    le = (jnp.einsum('td,ode->toe', ht, re_w) + re_b).astype(f32)
    le = jnp.take_along_axis(le, g_idx[:, :, None], axis=1)[:, 0]
    pe = jax.nn.softmax(le, axis=-1)
    e_val, e_idx = lax.top_k(pe, TOPK_EXPERT)
    e_val = e_val / jnp.sum(e_val, axis=-1, keepdims=True)
    w = g_val * e_val
    coef = jnp.sum(jax.nn.one_hot(e_idx, EXPERTS_PER_GROUP, dtype=f32) * w[..., None], axis=1)
    out = jnp.zeros_like(ht)
    for o in range(N_GROUPS):
        c_o = jnp.where(g_idx == o, coef, 0.0).astype(h.dtype)
        hid = jax.nn.silu(jnp.einsum('td,edf->tef', ht, w_gate[o])) * jnp.einsum('td,edf->tef', ht, w_up[o])
        out = out + jnp.einsum('tef,efd->td', hid * c_o[:, :, None], w_down[o])
    return out.reshape(bsz, seq, d)


def setup_inputs(seed: int = 0) -> dict:
    key = jax.random.key(seed)
    ks = jax.random.split(key, 28)

    def nrm(k, shape, scale):
        return scale * jax.random.normal(k, shape, dtype=jnp.float32)

    L = DEPTH
    G, P, H = N_SSM_GROUPS, SSM_STATE, SSM_GROUP
    return {
        'x': nrm(ks[0], (BATCH, SEQ, D_MODEL), 1.0),
        'norm1_g': 1.0 + nrm(ks[1], (L, D_MODEL), 0.02),
        'norm2_g': 1.0 + nrm(ks[2], (L, D_MODEL), 0.02),
        'final_g': 1.0 + nrm(ks[3], (D_MODEL,), 0.02),
        'w_in': nrm(ks[4], (L, D_MODEL, D_IN), D_MODEL ** -0.5),
        'w_out': nrm(ks[5], (L, D_SSM + D_ATTN, D_MODEL), (D_SSM + D_ATTN) ** -0.5),
        'ssm_a_re': -0.5 + nrm(ks[6], (L, G, P), 0.01),
        'ssm_a_im': math.pi * jnp.arange(P, dtype=jnp.float32)[None, None, :] + nrm(ks[7], (L, G, P), 0.01),
        'ssm_b_re': nrm(ks[8], (L, G, P, H), (0.5 / H) ** 0.5),
        'ssm_b_im': nrm(ks[9], (L, G, P, H), (0.5 / H) ** 0.5),
        'ssm_c_re': nrm(ks[10], (L, G, H, P), (0.5 / P) ** 0.5),
        'ssm_c_im': nrm(ks[11], (L, G, H, P), (0.5 / P) ** 0.5),
        'ssm_d': nrm(ks[12], (L, G, H), 1.0),
        'ssm_log_dt': jax.random.uniform(ks[13], (L, G), dtype=jnp.float32,
                                         minval=math.log(1e-3), maxval=math.log(1e-1)),
        'ssm_w_glu': nrm(ks[14], (L, D_SSM, 2 * D_SSM), D_SSM ** -0.5),
        'cmp_pos': nrm(ks[15], (L, 2, CMP_BLOCK, HEAD_DIM), 0.02),
        'cmp_w1': nrm(ks[16], (L, 2, CMP_BLOCK, HEAD_DIM, HEAD_DIM), (CMP_BLOCK * HEAD_DIM) ** -0.5),
        'cmp_w2': nrm(ks[17], (L, 2, HEAD_DIM, HEAD_DIM), HEAD_DIM ** -0.5),
        'rel_bias': nrm(ks[18], (N_BUCKETS, N_HEADS), 0.5),
        'router_g_w': nrm(ks[19], (L, D_MODEL, N_GROUPS), D_MODEL ** -0.5),
        'router_g_b': nrm(ks[20], (L, N_GROUPS), 0.01),
        'router_e_w': nrm(ks[21], (L, N_GROUPS, D_MODEL, EXPERTS_PER_GROUP), D_MODEL ** -0.5),
        'router_e_b': nrm(ks[22], (L, N_GROUPS, EXPERTS_PER_GROUP), 0.01),
        'exp_w_gate': nrm(ks[23], (L, N_GROUPS, EXPERTS_PER_GROUP, D_MODEL, D_EXPERT), D_MODEL ** -0.5),
        'exp_w_up': nrm(ks[24], (L, N_GROUPS, EXPERTS_PER_GROUP, D_MODEL, D_EXPERT), D_MODEL ** -0.5),
        'exp_w_down': nrm(ks[25], (L, N_GROUPS, EXPERTS_PER_GROUP, D_EXPERT, D_MODEL), D_EXPERT ** -0.5),
    }


def reference(x, norm1_g, norm2_g, final_g, w_in, w_out, ssm_a_re, ssm_a_im, ssm_b_re, ssm_b_im,
              ssm_c_re, ssm_c_im, ssm_d, ssm_log_dt, ssm_w_glu, cmp_pos, cmp_w1, cmp_w2, rel_bias,
              router_g_w, router_g_b, router_e_w, router_e_b, exp_w_gate, exp_w_up, exp_w_down):
    o_q = D_SSM
    o_kv = D_SSM + D_ATTN
    o_g = o_kv + 6 * D_KV
    for l in range(DEPTH):
        h = _rmsnorm(x, norm1_g[l])
        z = h @ w_in[l]
        u_ssm = z[..., :o_q]
        q = z[..., o_q:o_kv]
        k_cmp, v_cmp, k_sel, v_sel, k_win, v_win = jnp.split(z[..., o_kv:o_g], 6, axis=-1)
        gates = z[..., o_g:]
        y_ssm = _s5_mixer(u_ssm, ssm_a_re[l], ssm_a_im[l], ssm_b_re[l], ssm_b_im[l],
                          ssm_c_re[l], ssm_c_im[l], ssm_d[l], ssm_log_dt[l], ssm_w_glu[l])
        y_attn = _nsa_mixer(q, k_cmp, v_cmp, k_sel, v_sel, k_win, v_win, gates,
                            cmp_pos[l], cmp_w1[l], cmp_w2[l], rel_bias)
        x = x + jnp.concatenate([y_ssm, y_attn], axis=-1) @ w_out[l]
        h = _rmsnorm(x, norm2_g[l])
        x = x + _hier_moe(h, router_g_w[l], router_g_b[l], router_e_w[l], router_e_b[l],
                          exp_w_gate[l], exp_w_up[l], exp_w_down[l])
    return _rmsnorm(x, final_g)
```

```python
import functools
import math

import numpy as np
import jax
import jax.numpy as jnp
from jax import lax
from jax.experimental import pallas as pl
from jax.experimental.pallas import tpu as pltpu

F32 = jnp.float32
BF16 = jnp.bfloat16

SSM_GROUP = 16
SSM_STATE = 64
HEAD_DIM = 64
N_KV_HEADS = 2
GQA = 4
N_HEADS = N_KV_HEADS * GQA
CMP_BLOCK = 32
CMP_STRIDE = 16
SEL_BLOCK = 64
SEL_TOPK = 16
N_LOCAL_BLOCKS = 2
WINDOW = 512
N_BUCKETS = 32
MAX_DISTANCE = 128
N_GROUPS = 4
EXPERTS_PER_GROUP = 8
N_EXPERTS = N_GROUPS * EXPERTS_PER_GROUP
EPS = 1e-6
NEG = -1e30
BIG = 1e9

LANES = 128
SUBLANES = 8
ATT_TILE = 128
SSM_OCT = 8
VMEM_LIMIT = 56 * 1024 * 1024

HIGHEST = lax.Precision.HIGHEST


def _cparams(*sem):
    return pltpu.CompilerParams(dimension_semantics=sem, vmem_limit_bytes=VMEM_LIMIT)


def _dot(a, b, precision=None):
    return jnp.dot(a, b, preferred_element_type=F32, precision=precision)


def _dot_nt(a, b, precision=None):
    return lax.dot_general(a, b, (((1,), (1,)), ((), ())),
                           preferred_element_type=F32, precision=precision)


def _gelu_tanh(x):
    return 0.5 * x * (1.0 + jnp.tanh(math.sqrt(2.0 / math.pi) * (x + 0.044715 * (x * x * x))))


def _sigmoid(x):
    return 1.0 / (1.0 + jnp.exp(-x))


def _t5_bucket_np(dist):
    n = np.maximum(dist, 0)
    max_exact = N_BUCKETS // 2
    nf = np.maximum(n, 1).astype(np.float32)
    large = max_exact + (np.log(nf / np.float32(max_exact))
                         / np.float32(math.log(MAX_DISTANCE / max_exact))
                         * np.float32(N_BUCKETS - max_exact)).astype(np.int32)
    large = np.minimum(large, N_BUCKETS - 1)
    return np.where(n < max_exact, n, large).astype(np.int32)


def _inproj_kernel(x_ref, g_ref, wu_ref, wq_ref, wc_ref, wsw_ref, wg_ref,
                   u_ref, q_ref, kc_ref, vc_ref, ks_ref, vs_ref, kw_ref, vw_ref, gate_ref):
    x = x_ref[0]
    ms = jnp.mean(x * x, axis=-1, keepdims=True)
    h = (x * lax.rsqrt(ms + EPS) * g_ref[...]).astype(BF16)
    u_ref[...] = _dot(h, wu_ref[...])
    q = (_dot(h, wq_ref[...]) * (HEAD_DIM ** -0.5)).astype(BF16)
    for hd in range(N_HEADS):
        q_ref[0, hd] = q[:, hd * HEAD_DIM:(hd + 1) * HEAD_DIM]
    c = _dot(h, wc_ref[...])
    dkv = N_KV_HEADS * HEAD_DIM
    kc_ref[0] = c[:, :dkv]
    vc_ref[0] = c[:, dkv:]
    sw = _dot(h, wsw_ref[...]).astype(BF16)
    for i, ref in enumerate((ks_ref, vs_ref, kw_ref, vw_ref)):
        for hk in range(N_KV_HEADS):
            lo = i * dkv + hk * HEAD_DIM
            ref[0, hk] = sw[:, lo:lo + HEAD_DIM]
    gate_ref[0] = _dot(h, wg_ref[...])


def _inproj(x, g, w_in, tm):
    bsz, seq, d = x.shape
    d_ssm = d // 2
    d_attn = d - d_ssm
    dkv = N_KV_HEADS * HEAD_DIM
    o_kv = d_ssm + d_attn
    o_g = o_kv + 6 * dkv
    wb = w_in.astype(BF16)
    wu = wb[:, :d_ssm]
    wq = wb[:, d_ssm:o_kv]
    wc = wb[:, o_kv:o_kv + 2 * dkv]
    wsw = wb[:, o_kv + 2 * dkv:o_g]
    wg = jnp.pad(wb[:, o_g:], ((0, 0), (0, LANES - 3 * N_HEADS)))
    full = lambda shape: pl.BlockSpec(shape, lambda b, i: (0,) * len(shape))
    head_spec = lambda nh: pl.BlockSpec((1, nh, tm, HEAD_DIM), lambda b, i: (b, 0, i, 0))
    kv_shape = jax.ShapeDtypeStruct((bsz, N_KV_HEADS, seq, HEAD_DIM), BF16)
    return pl.pallas_call(
        _inproj_kernel,
        grid=(bsz, seq // tm),
        in_specs=[pl.BlockSpec((1, tm, d), lambda b, i: (b, i, 0)),
                  full((1, d)), full(wu.shape), full(wq.shape), full(wc.shape),
                  full(wsw.shape), full(wg.shape)],
        out_specs=[pl.BlockSpec((tm, d_ssm), lambda b, i: (i, b)),
                   head_spec(N_HEADS),
                   pl.BlockSpec((1, tm, dkv), lambda b, i: (b, i, 0)),
                   pl.BlockSpec((1, tm, dkv), lambda b, i: (b, i, 0)),
                   head_spec(N_KV_HEADS), head_spec(N_KV_HEADS),
                   head_spec(N_KV_HEADS), head_spec(N_KV_HEADS),
                   pl.BlockSpec((1, tm, LANES), lambda b, i: (b, i, 0))],
        out_shape=[jax.ShapeDtypeStruct((seq, bsz * d_ssm), F32),
                   jax.ShapeDtypeStruct((bsz, N_HEADS, seq, HEAD_DIM), BF16),
                   jax.ShapeDtypeStruct((bsz, seq, dkv), F32),
                   jax.ShapeDtypeStruct((bsz, seq, dkv), F32),
                   kv_shape, kv_shape, kv_shape, kv_shape,
                   jax.ShapeDtypeStruct((bsz, seq, LANES), F32)],
        compiler_params=_cparams("parallel", "parallel"),
        name="inproj",
    )(x, g.reshape(1, d), wu, wq, wc, wsw, wg)


def _s5_kernel(u_ref, bmat_ref, cmat_ref, lam_ref, d_ref, wglu_ref, y_ref, bu_ref, st_ref,
               *, tc, bsz, n_oct, oct_c, oct_s, lane_chunk):
    rows = tc * bsz
    d_ssm = n_oct * oct_c
    n_state = n_oct * oct_s

    @pl.when(pl.program_id(0) == 0)
    def _():
        st_ref[...] = jnp.zeros_like(st_ref)

    u = u_ref[...].reshape(rows, d_ssm)
    ub = u.astype(BF16)
    for o in range(n_oct):
        r = _dot(ub[:, o * oct_c:(o + 1) * oct_c], bmat_ref[o])
        bu_ref[:, o * oct_s:(o + 1) * oct_s] = r[:, :oct_s]
        bu_ref[:, n_state + o * oct_s:n_state + (o + 1) * oct_s] = r[:, oct_s:]

    for c0 in range(0, n_state, lane_chunk):
        re = slice(c0, c0 + lane_chunk)
        im = slice(n_state + c0, n_state + c0 + lane_chunk)

        def step(t, carry, re=re, im=im):
            sr, si = carry
            row = pl.multiple_of(t * bsz, bsz)
            lr = lam_ref[:, re]
            li = lam_ref[:, im]
            nr = lr * sr - li * si + bu_ref[pl.ds(row, bsz), re]
            ni = lr * si + li * sr + bu_ref[pl.ds(row, bsz), im]
            bu_ref[pl.ds(row, bsz), re] = nr
            bu_ref[pl.ds(row, bsz), im] = ni
            return nr, ni

        sr, si = lax.fori_loop(0, tc, step, (st_ref[:, re], st_ref[:, im]))
        st_ref[:, re] = sr
        st_ref[:, im] = si

    ys = []
    for o in range(n_oct):
        s_re = bu_ref[:, o * oct_s:(o + 1) * oct_s].astype(BF16)
        s_im = bu_ref[:, n_state + o * oct_s:n_state + (o + 1) * oct_s].astype(BF16)
        ys.append(_dot(s_re, cmat_ref[o, :oct_s]) + _dot(s_im, cmat_ref[o, oct_s:]))
    y = jnp.concatenate(ys, axis=1) + d_ref[...] * u
    y = _gelu_tanh(y).astype(BF16)
    z = _dot(y, wglu_ref[...])
    out = z[:, :d_ssm] * _sigmoid(z[:, d_ssm:])
    y_ref[...] = out.astype(BF16).reshape(tc, bsz, d_ssm)


def _s5_params(a_re, a_im, b_re, b_im, c_re, c_im, log_dt, bsz):
    g, p = a_re.shape
    h = b_re.shape[-1]
    n_oct = g // SSM_OCT
    dt = jnp.exp(log_dt)[:, None]
    mag = jnp.exp(a_re * dt)
    lr = mag * jnp.cos(a_im * dt)
    li = mag * jnp.sin(a_im * dt)
    den = a_re * a_re + a_im * a_im
    kr = ((lr - 1.0) * a_re + li * a_im) / den
    ki = (li * a_re - (lr - 1.0) * a_im) / den
    bb_re = kr[..., None] * b_re - ki[..., None] * b_im
    bb_im = kr[..., None] * b_im + ki[..., None] * b_re
    eye = jnp.eye(SSM_OCT, dtype=F32)

    def bd_in(m):
        m = m.reshape(n_oct, SSM_OCT, p, h)
        return jnp.einsum('ogph,gk->oghkp', m, eye).reshape(n_oct, SSM_OCT * h, SSM_OCT * p)

    def bd_out(m):
        m = m.reshape(n_oct, SSM_OCT, h, p)
        return jnp.einsum('oghp,gk->ogpkh', m, eye).reshape(n_oct, SSM_OCT * p, SSM_OCT * h)

    bmat = jnp.concatenate([bd_in(bb_re), bd_in(bb_im)], axis=-1).astype(BF16)
    cmat = jnp.concatenate([bd_out(c_re), bd_out(-c_im)], axis=1).astype(BF16)
    lam = jnp.concatenate([lr.reshape(1, g * p), li.reshape(1, g * p)], axis=1)
    lam = jnp.broadcast_to(lam, (bsz, 2 * g * p))
    return bmat, cmat, lam


def _s5(u_tm, bsz, a_re, a_im, b_re, b_im, c_re, c_im, d, log_dt, w_glu, tc):
    seq = u_tm.shape[0]
    d_ssm = u_tm.shape[1] // bsz
    g, p = a_re.shape
    n_oct = g // SSM_OCT
    oct_c = SSM_OCT * SSM_GROUP
    oct_s = SSM_OCT * p
    bmat, cmat, lam = _s5_params(a_re, a_im, b_re, b_im, c_re, c_im, log_dt, bsz)
    u3 = u_tm.reshape(seq, bsz, d_ssm)
    full = lambda shape: pl.BlockSpec(shape, lambda i: (0,) * len(shape))
    kern = functools.partial(_s5_kernel, tc=tc, bsz=bsz, n_oct=n_oct, oct_c=oct_c, oct_s=oct_s,
                             lane_chunk=min(1024, g * p))
    y = pl.pallas_call(
        kern,
        grid=(seq // tc,),
        in_specs=[pl.BlockSpec((tc, bsz, d_ssm), lambda i: (i, 0, 0)),
                  full(bmat.shape), full(cmat.shape), full(lam.shape),
                  full((1, d_ssm)), full(w_glu.shape)],
        out_specs=pl.BlockSpec((tc, bsz, d_ssm), lambda i: (i, 0, 0)),
        out_shape=jax.ShapeDtypeStruct((seq, bsz, d_ssm), BF16),
        scratch_shapes=[pltpu.VMEM((tc * bsz, 2 * g * p), F32),
                        pltpu.VMEM((bsz, 2 * g * p), F32)],
        compiler_params=_cparams("arbitrary"),
        name="s5_mixer",
    )(u3, bmat, cmat, lam, d.reshape(1, d_ssm), w_glu.astype(BF16))
    return y.reshape(seq, bsz * d_ssm)


def _compress_kernel(k_ref, v_ref, w1a_ref, w1b_ref, posw_ref, w2_ref, kc_ref, vc_ref):
    for i, (src, dst) in enumerate(((k_ref, kc_ref), (v_ref, vc_ref))):
        x = src[0]
        a = _dot(x, w1a_ref[i], HIGHEST)
        b = _dot(x, w1b_ref[i], HIGHEST)
        n = a.shape[0]
        hid = _gelu_tanh(a + pltpu.roll(b, n - 1, 0) + posw_ref[i])
        out = _dot(hid, w2_ref[i], HIGHEST).astype(BF16)
        for hk in range(N_KV_HEADS):
            dst[0, hk] = out[:, hk * HEAD_DIM:(hk + 1) * HEAD_DIM]


def _compress(k_cmp, v_cmp, cmp_pos, cmp_w1, cmp_w2):
    bsz, seq, dkv = k_cmp.shape
    n_chunk = seq // CMP_STRIDE
    kview = k_cmp.reshape(bsz, n_chunk, CMP_STRIDE * dkv)
    vview = v_cmp.reshape(bsz, n_chunk, CMP_STRIDE * dkv)
    eye = jnp.eye(N_KV_HEADS, dtype=F32)
    w1 = jnp.einsum('ilde,hk->ilhdke', cmp_w1, eye)
    w1 = w1.reshape(2, CMP_BLOCK, dkv, dkv)
    w1a = w1[:, :CMP_STRIDE].reshape(2, CMP_STRIDE * dkv, dkv)
    w1b = w1[:, CMP_STRIDE:].reshape(2, CMP_STRIDE * dkv, dkv)
    posw = jnp.einsum('ild,ilde->ie', cmp_pos, cmp_w1, precision=HIGHEST)
    posw = jnp.tile(posw, (1, N_KV_HEADS)).reshape(2, 1, dkv)
    w2 = jnp.einsum('ief,hk->ihekf', cmp_w2, eye).reshape(2, dkv, dkv)
    full = lambda shape: pl.BlockSpec(shape, lambda b: (0,) * len(shape))
    out_shape = jax.ShapeDtypeStruct((bsz, N_KV_HEADS, n_chunk, HEAD_DIM), BF16)
    return pl.pallas_call(
        _compress_kernel,
        grid=(bsz,),
        in_specs=[pl.BlockSpec((1, n_chunk, CMP_STRIDE * dkv), lambda b: (b, 0, 0)),
                  pl.BlockSpec((1, n_chunk, CMP_STRIDE * dkv), lambda b: (b, 0, 0)),
                  full(w1a.shape), full(w1b.shape), full(posw.shape), full(w2.shape)],
        out_specs=[pl.BlockSpec((1, N_KV_HEADS, n_chunk, HEAD_DIM), lambda b: (b, 0, 0, 0))] * 2,
        out_shape=[out_shape, out_shape],
        compiler_params=_cparams("parallel"),
        name="compress",
    )(kview, vview, w1a, w1b, posw, w2)


def _cmp_attn_kernel(q_ref, kc_ref, vc_ref, bias_ref, ovl_ref, ocmp_ref, sel_ref, *, tq, n_sel, n_top):
    qi = pl.program_id(0)
    n_chunk = kc_ref.shape[2]
    t = qi * tq + lax.broadcasted_iota(jnp.int32, (tq, n_chunk), 0)
    n = lax.broadcasted_iota(jnp.int32, (tq, n_chunk), 1)
    valid = t >= n * CMP_STRIDE + (CMP_BLOCK - 1)
    jrow = lax.broadcasted_iota(jnp.int32, (n_sel, tq), 0)
    blk_t = (qi * tq + lax.broadcasted_iota(jnp.int32, (n_sel, tq), 1)) // SEL_BLOCK
    forced = (jrow == 0) | ((jrow <= blk_t) & (jrow > blk_t - N_LOCAL_BLOCKS))
    future = jrow > blk_t
    for hk in range(N_KV_HEADS):
        kc = kc_ref[0, hk]
        vc = vc_ref[0, hk]
        psum = jnp.zeros((tq, n_chunk), F32)
        for g in range(GQA):
            hd = hk * GQA + g
            s = _dot_nt(q_ref[0, hd], kc) + bias_ref[hd]
            s = jnp.where(valid, s, NEG)
            m = jnp.max(s, axis=-1, keepdims=True)
            e = jnp.where(valid, jnp.exp(s - m), 0.0)
            l = jnp.sum(e, axis=-1, keepdims=True)
            p = e * jnp.where(l > 0.0, 1.0 / l, 0.0)
            psum = psum + p
            ocmp_ref[0, hd] = _dot(p.astype(BF16), vc)
        imp = _dot_nt(ovl_ref[...], psum, HIGHEST)
        v = jnp.where(forced, BIG, jnp.where(future, -BIG, imp))
        rank = jnp.zeros((n_sel, tq), F32)
        for jp in range(n_sel):
            row = v[jp:jp + 1, :]
            rank = rank + jnp.where(jrow > jp, jnp.where(row >= v, 1.0, 0.0), jnp.where(row > v, 1.0, 0.0))
        sel = jnp.where(rank < float(n_top), 1.0, 0.0)
        pad = jnp.zeros((LANES - n_sel, tq), F32) if n_sel < LANES else None
        selp = sel if pad is None else jnp.concatenate([sel, pad], axis=0)
        sel_ref[0, hk] = selp.T[:, :n_sel].astype(BF16)


def _cmp_bias_table(rel_bias, seq, n_chunk):
    t = np.arange(seq)[:, None]
    end = np.arange(n_chunk)[None, :] * CMP_STRIDE + (CMP_BLOCK - 1)
    idx = jnp.asarray(_t5_bucket_np(t - end))
    return jnp.transpose(rel_bias[idx], (2, 0, 1))


def _cmp_attn(q, kc, vc, bias_c, tq):
    bsz, _, seq, _ = q.shape
    n_chunk = kc.shape[2]
    n_sel = seq // SEL_BLOCK
    n_top = min(SEL_TOPK, n_sel)
    n_cmp = (seq - CMP_BLOCK) // CMP_STRIDE + 1
    cs = np.arange(n_chunk)[:, None] * CMP_STRIDE
    ss = np.arange(n_sel)[None, :] * SEL_BLOCK
    ovl = np.clip(np.minimum(cs + CMP_BLOCK, ss + SEL_BLOCK) - np.maximum(cs, ss), 0, None) / CMP_BLOCK
    ovl[n_cmp:] = 0.0
    ovl_t = jnp.asarray(ovl.T, dtype=F32)
    kern = functools.partial(_cmp_attn_kernel, tq=tq, n_sel=n_sel, n_top=n_top)
    return pl.pallas_call(
        kern,
        grid=(seq // tq, bsz),
        in_specs=[pl.BlockSpec((1, N_HEADS, tq, HEAD_DIM), lambda i, b: (b, 0, i, 0)),
                  pl.BlockSpec((1, N_KV_HEADS, n_chunk, HEAD_DIM), lambda i, b: (b, 0, 0, 0)),
                  pl.BlockSpec((1, N_KV_HEADS, n_chunk, HEAD_DIM), lambda i, b: (b, 0, 0, 0)),
                  pl.BlockSpec((N_HEADS, tq, n_chunk), lambda i, b: (0, i, 0)),
                  pl.BlockSpec((n_sel, n_chunk), lambda i, b: (0, 0))],
        out_specs=[pl.BlockSpec((1, N_HEADS, tq, HEAD_DIM), lambda i, b: (b, 0, i, 0)),
                   pl.BlockSpec((1, N_KV_HEADS, tq, n_sel), lambda i, b: (b, 0, i, 0))],
        out_shape=[jax.ShapeDtypeStruct((bsz, N_HEADS, seq, HEAD_DIM), F32),
                   jax.ShapeDtypeStruct((bsz, N_KV_HEADS, seq, n_sel), BF16)],
        compiler_params=_cparams("parallel", "parallel"),
        name="cmp_attn",
    )(q, kc, vc, bias_c, ovl_t)


def _sw_attn_kernel(r31_ref, q_ref, ks_ref, vs_ref, kw_ref, vw_ref, sel_ref, eoh_ref, pd_ref, p1_ref,
                    ocmp_ref, gate_ref, o_ref, *, tq, n_win_tiles):
    qi = pl.program_id(1)
    rows = GQA * tq
    row_i = lax.broadcasted_iota(jnp.int32, (tq, ATT_TILE), 0)
    col_i = lax.broadcasted_iota(jnp.int32, (tq, ATT_TILE), 1)
    causal = col_i <= row_i
    gates = _sigmoid(gate_ref[0])
    g_of_row = lax.broadcasted_iota(jnp.int32, (GQA, tq, 1), 0)

    def tile_update(carry, q4, k, v, bias, valid):
        m, l, acc = carry
        s = _dot_nt(q4, k).reshape(GQA, tq, ATT_TILE) + bias
        s = jnp.where(valid[None], s, NEG)
        m_new = jnp.maximum(m, jnp.max(s, axis=-1, keepdims=True))
        alpha = jnp.exp(m - m_new)
        p = jnp.where(valid[None], jnp.exp(s - m_new), 0.0)
        l = alpha * l + jnp.sum(p, axis=-1, keepdims=True)
        pv = _dot(p.reshape(rows, ATT_TILE).astype(BF16), v).reshape(GQA, tq, HEAD_DIM)
        return m_new, l, alpha * acc + pv

    outs = []
    for hk in range(N_KV_HEADS):
        q4 = q_ref[0, hk * GQA:(hk + 1) * GQA].reshape(rows, HEAD_DIM)
        far_bias = jnp.zeros((GQA, tq, 1), F32)
        for g in range(GQA):
            far_bias = jnp.where(g_of_row == g, r31_ref[hk * GQA + g], far_bias)
        pd = pd_ref[hk * GQA:(hk + 1) * GQA]
        p1 = p1_ref[hk * GQA:(hk + 1) * GQA]
        selm = sel_ref[0, hk]

        def sel_valid(kt):
            e = eoh_ref[:, pl.ds(pl.multiple_of(kt * ATT_TILE, ATT_TILE), ATT_TILE)]
            return _dot(selm, e) > 0.5

        def k_tile(ref, kt):
            return ref[0, hk, pl.ds(pl.multiple_of(kt * ATT_TILE, ATT_TILE), ATT_TILE), :]

        init = (jnp.full((GQA, tq, 1), NEG, F32), jnp.zeros((GQA, tq, 1), F32),
                jnp.zeros((GQA, tq, HEAD_DIM), F32))

        def far_step(kt, carry):
            return tile_update(carry, q4, k_tile(ks_ref, kt), k_tile(vs_ref, kt), far_bias, sel_valid(kt))

        carry = lax.fori_loop(0, jnp.maximum(qi - 1, 0), far_step, init)
        kt1 = jnp.maximum(qi - 1, 0)
        carry = tile_update(carry, q4, k_tile(ks_ref, kt1), k_tile(vs_ref, kt1), p1,
                            sel_valid(kt1) & (qi >= 1))
        m, l, acc = tile_update(carry, q4, k_tile(ks_ref, qi), k_tile(vs_ref, qi), pd,
                                sel_valid(qi) & causal)
        o_sel = acc / l

        carry = init
        for w in range(n_win_tiles + 1):
            back = n_win_tiles - w
            kt = jnp.maximum(qi - back, 0)
            ok = qi >= back
            if back == 0:
                bias, valid = pd, causal
            elif back == n_win_tiles:
                bias, valid = (p1 if back == 1 else far_bias), (col_i > row_i) & ok
            else:
                bias, valid = (p1 if back == 1 else far_bias), jnp.full((tq, ATT_TILE), True) & ok
            carry = tile_update(carry, q4, k_tile(kw_ref, kt), k_tile(vw_ref, kt), bias, valid)
        m, l, acc = carry
        o_win = acc / l

        for g in range(GQA):
            hd = hk * GQA + g
            outs.append(gates[:, 3 * hd:3 * hd + 1] * ocmp_ref[0, hd]
                        + gates[:, 3 * hd + 1:3 * hd + 2] * o_sel[g]
                        + gates[:, 3 * hd + 2:3 * hd + 3] * o_win[g])
    o_ref[0] = jnp.concatenate(outs, axis=1).astype(BF16)


def _near_bias_tables(rel_bias):
    i = np.arange(ATT_TILE)[:, None]
    j = np.arange(ATT_TILE)[None, :]
    pd = rel_bias[jnp.asarray(_t5_bucket_np(i - j))]
    p1 = rel_bias[jnp.asarray(_t5_bucket_np(ATT_TILE + i - j))]
    return jnp.transpose(pd, (2, 0, 1)), jnp.transpose(p1, (2, 0, 1))


def _sw_attn(q, ks, vs, kw, vw, sel, ocmp, gates, rel_bias, tq):
    bsz, _, seq, _ = q.shape
    n_sel = seq // SEL_BLOCK
    assert tq == ATT_TILE and WINDOW % ATT_TILE == 0
    assert int(_t5_bucket_np(np.arange(ATT_TILE + 1, seq)).min()) == N_BUCKETS - 1
    pd, p1 = _near_bias_tables(rel_bias)
    r31 = rel_bias[N_BUCKETS - 1]
    eoh = np.zeros((n_sel, seq), np.float32)
    eoh[np.arange(seq) // SEL_BLOCK, np.arange(seq)] = 1.0
    eoh = jnp.asarray(eoh, dtype=BF16)
    kern = functools.partial(_sw_attn_kernel, tq=tq, n_win_tiles=WINDOW // ATT_TILE)
    kv_spec = pl.BlockSpec((1, N_KV_HEADS, seq, HEAD_DIM), lambda b, i: (b, 0, 0, 0))
    head_spec = pl.BlockSpec((1, N_HEADS, tq, HEAD_DIM), lambda b, i: (b, 0, i, 0))
    full = lambda shape: pl.BlockSpec(shape, lambda b, i: (0,) * len(shape))
    return pl.pallas_call(
        kern,
        grid=(bsz, seq // tq),
        in_specs=[pl.BlockSpec(memory_space=pltpu.SMEM),
                  head_spec, kv_spec, kv_spec, kv_spec, kv_spec,
                  pl.BlockSpec((1, N_KV_HEADS, tq, n_sel), lambda b, i: (b, 0, i, 0)),
                  full(eoh.shape), full(pd.shape), full(p1.shape),
                  head_spec,
                  pl.BlockSpec((1, tq, LANES), lambda b, i: (b, i, 0))],
        out_specs=pl.BlockSpec((1, tq, N_HEADS * HEAD_DIM), lambda b, i: (b, i, 0)),
        out_shape=jax.ShapeDtypeStruct((bsz, seq, N_HEADS * HEAD_DIM), BF16),
        compiler_params=_cparams("parallel", "parallel"),
        name="sel_win_attn",
    )(r31, q, ks, vs, kw, vw, sel, eoh, pd, p1, ocmp, gates)


def _outproj_kernel(x_ref, ys_ref, ya_ref, ws_ref, wa_ref, o_ref):
    o_ref[0] = x_ref[0] + _dot(ys_ref[...], ws_ref[...]) + _dot(ya_ref[0], wa_ref[...])


def _outproj(x, y_ssm_tm, y_attn, w_out, tm):
    bsz, seq, d = x.shape
    d_ssm = y_ssm_tm.shape[1] // bsz
    d_attn = y_attn.shape[2]
    wb = w_out.astype(BF16)
    ws, wa = wb[:d_ssm], wb[d_ssm:]
    full = lambda shape: pl.BlockSpec(shape, lambda b, i: (0,) * len(shape))
    return pl.pallas_call(
        _outproj_kernel,
        grid=(bsz, seq // tm),
        in_specs=[pl.BlockSpec((1, tm, d), lambda b, i: (b, i, 0)),
                  pl.BlockSpec((tm, d_ssm), lambda b, i: (i, b)),
                  pl.BlockSpec((1, tm, d_attn), lambda b, i: (b, i, 0)),
                  full(ws.shape), full(wa.shape)],
        out_specs=pl.BlockSpec((1, tm, d), lambda b, i: (b, i, 0)),
        out_shape=jax.ShapeDtypeStruct((bsz, seq, d), F32),
        compiler_params=_cparams("parallel", "parallel"),
        name="outproj",
    )(x, y_ssm_tm, y_attn, ws, wa)


def _router_kernel(x_ref, g_ref, wr_ref, br_ref, h_ref, coef_ref):
    x = x_ref[...]
    ms = jnp.mean(x * x, axis=-1, keepdims=True)
    h = x * lax.rsqrt(ms + EPS) * g_ref[...]
    h_ref[...] = h.astype(BF16)
    logits = _dot(h, wr_ref[...], HIGHEST) + br_ref[...]
    lane = lax.broadcasted_iota(jnp.int32, logits.shape, 1)
    is_grp = (lane >= N_EXPERTS) & (lane < N_EXPERTS + N_GROUPS)
    lg = jnp.where(is_grp, logits, NEG)
    mg = jnp.max(lg, axis=-1, keepdims=True)
    g_val = 1.0 / jnp.sum(jnp.where(is_grp, jnp.exp(lg - mg), 0.0), axis=-1, keepdims=True)
    g_idx = jnp.min(jnp.where(is_grp & (lg == mg), lane - N_EXPERTS, N_GROUPS), axis=-1, keepdims=True)
    in_grp = (lane >= g_idx * EXPERTS_PER_GROUP) & (lane < (g_idx + 1) * EXPERTS_PER_GROUP)
    le = jnp.where(in_grp, logits, NEG)
    me = jnp.max(le, axis=-1, keepdims=True)
    ee = jnp.where(in_grp, jnp.exp(le - me), 0.0)
    pe = ee / jnp.sum(ee, axis=-1, keepdims=True)
    p1 = jnp.max(pe, axis=-1, keepdims=True)
    i1 = jnp.min(jnp.where(in_grp & (pe == p1), lane, LANES), axis=-1, keepdims=True)
    rest = in_grp & (lane != i1)
    pr = jnp.where(rest, pe, -1.0)
    p2 = jnp.max(pr, axis=-1, keepdims=True)
    i2 = jnp.min(jnp.where(rest & (pr == p2), lane, LANES), axis=-1, keepdims=True)
    tot = p1 + p2
    coef_ref[...] = (jnp.where(lane == i1, g_val * (p1 / tot), 0.0)
                     + jnp.where(lane == i2, g_val * (p2 / tot), 0.0))


def _router(x2d, g, rg_w, rg_b, re_w, re_b, tm):
    tok, d = x2d.shape
    we = jnp.transpose(re_w, (1, 0, 2)).reshape(d, N_EXPERTS)
    wr = jnp.pad(jnp.concatenate([we, rg_w], axis=1), ((0, 0), (0, LANES - N_EXPERTS - N_GROUPS)))
    br = jnp.pad(jnp.concatenate([re_b.reshape(-1), rg_b]), (0, LANES - N_EXPERTS - N_GROUPS)).reshape(1, LANES)
    full = lambda shape: pl.BlockSpec(shape, lambda i: (0,) * len(shape))
    return pl.pallas_call(
        _router_kernel,
        grid=(tok // tm,),
        in_specs=[pl.BlockSpec((tm, d), lambda i: (i, 0)), full((1, d)), full(wr.shape), full(br.shape)],
        out_specs=[pl.BlockSpec((tm, d), lambda i: (i, 0)), pl.BlockSpec((tm, LANES), lambda i: (i, 0))],
        out_shape=[jax.ShapeDtypeStruct((tok, d), BF16), jax.ShapeDtypeStruct((tok, LANES), F32)],
        compiler_params=_cparams("parallel"),
        name="router",
    )(x2d, g.reshape(1, d), wr, br)


def _moe_kernel(h_ref, coef_ref, x_ref, wg_ref, wu_ref, wd_ref, o_ref):
    e = pl.program_id(1)

    @pl.when(e == 0)
    def _():
        o_ref[...] = x_ref[...]

    h = h_ref[...]
    coef = coef_ref[...]
    lane = lax.broadcasted_iota(jnp.int32, coef.shape, 1)
    c = jnp.sum(jnp.where(lane == e, coef, 0.0), axis=-1, keepdims=True)
    a = _dot(h, wg_ref[0])
    hid = (a * _sigmoid(a)) * _dot(h, wu_ref[0]) * c
    o_ref[...] += _dot(hid.astype(BF16), wd_ref[0])


def _moe(h, coef, x2d, w_gate, w_up, w_down, tm):
    tok, d = x2d.shape
    f = w_gate.shape[-1]
    wg = w_gate.reshape(N_EXPERTS, d, f).astype(BF16)
    wu = w_up.reshape(N_EXPERTS, d, f).astype(BF16)
    wd = w_down.reshape(N_EXPERTS, f, d).astype(BF16)
    return pl.pallas_call(
        _moe_kernel,
        grid=(tok // tm, N_EXPERTS),
        in_specs=[pl.BlockSpec((tm, d), lambda i, e: (i, 0)),
                  pl.BlockSpec((tm, LANES), lambda i, e: (i, 0)),
                  pl.BlockSpec((tm, d), lambda i, e: (i, 0)),
                  pl.BlockSpec((1, d, f), lambda i, e: (e, 0, 0)),
                  pl.BlockSpec((1, d, f), lambda i, e: (e, 0, 0)),
                  pl.BlockSpec((1, f, d), lambda i, e: (e, 0, 0))],
        out_specs=pl.BlockSpec((tm, d), lambda i, e: (i, 0)),
        out_shape=jax.ShapeDtypeStruct((tok, d), F32),
        compiler_params=_cparams("parallel", "arbitrary"),
        name="moe_experts",
    )(h, coef, x2d, wg, wu, wd)


def _norm_kernel(x_ref, g_ref, o_ref):
    x = x_ref[...]
    ms = jnp.mean(x * x, axis=-1, keepdims=True)
    o_ref[...] = x * lax.rsqrt(ms + EPS) * g_ref[...]


def _final_norm(x2d, g, tm):
    tok, d = x2d.shape
    return pl.pallas_call(
        _norm_kernel,
        grid=(tok // tm,),
        in_specs=[pl.BlockSpec((tm, d), lambda i: (i, 0)), pl.BlockSpec((1, d), lambda i: (0, 0))],
        out_specs=pl.BlockSpec((tm, d), lambda i: (i, 0)),
        out_shape=jax.ShapeDtypeStruct((tok, d), F32),
        compiler_params=_cparams("parallel"),
        name="final_norm",
    )(x2d, g.reshape(1, d))


def kernel(x, norm1_g, norm2_g, final_g, w_in, w_out, ssm_a_re, ssm_a_im, ssm_b_re, ssm_b_im, ssm_c_re, ssm_c_im, ssm_d, ssm_log_dt, ssm_w_glu, cmp_pos, cmp_w1, cmp_w2, rel_bias, router_g_w, router_g_b, router_e_w, router_e_b, exp_w_gate, exp_w_up, exp_w_down):
    bsz, seq, d = x.shape
    depth = norm1_g.shape[0]
    tm = min(512, seq)
    tok_tile = min(1024, bsz * seq)
    n_chunk = seq // CMP_STRIDE
    bias_c = _cmp_bias_table(rel_bias, seq, n_chunk)
    for l in range(depth):
        u_tm, q, k_cmp, v_cmp, ks, vs, kw, vw, gates = _inproj(x, norm1_g[l], w_in[l], tm)
        y_ssm = _s5(u_tm, bsz, ssm_a_re[l], ssm_a_im[l], ssm_b_re[l], ssm_b_im[l], ssm_c_re[l],
                    ssm_c_im[l], ssm_d[l].reshape(-1), ssm_log_dt[l], ssm_w_glu[l], tc=min(32, seq))
        kc, vc = _compress(k_cmp, v_cmp, cmp_pos[l], cmp_w1[l], cmp_w2[l])
        ocmp, sel = _cmp_attn(q, kc, vc, bias_c, ATT_TILE)
        y_attn = _sw_attn(q, ks, vs, kw, vw, sel, ocmp, gates, rel_bias, ATT_TILE)
        x = _outproj(x, y_ssm, y_attn, w_out[l], tm)
        x2d = x.reshape(bsz * seq, d)
        h, coef = _router(x2d, norm2_g[l], router_g_w[l], router_g_b[l], router_e_w[l], router_e_b[l], tok_tile)
        x = _moe(h, coef, x2d, exp_w_gate[l], exp_w_up[l], exp_w_down[l], tok_tile).reshape(bsz, seq, d)
    return _final_norm(x.reshape(bsz * seq, d), final_g, tok_tile).reshape(bsz, seq, d)
```

```python
import functools
import math

import numpy as np
import jax
import jax.numpy as jnp
from jax import lax
from jax.experimental import pallas as pl
from jax.experimental.pallas import tpu as pltpu

F32 = jnp.float32
BF16 = jnp.bfloat16

SSM_GROUP = 16
SSM_STATE = 64
HEAD_DIM = 64
N_KV_HEADS = 2
GQA = 4
N_HEADS = N_KV_HEADS * GQA
CMP_BLOCK = 32
CMP_STRIDE = 16
SEL_BLOCK = 64
SEL_TOPK = 16
N_LOCAL_BLOCKS = 2
WINDOW = 512
N_BUCKETS = 32
MAX_DISTANCE = 128
N_GROUPS = 4
EXPERTS_PER_GROUP = 8
N_EXPERTS = N_GROUPS * EXPERTS_PER_GROUP
EPS = 1e-6
NEG = -1e30
BIG = 1e9

LANES = 128
SUBLANES = 8
ATT_TILE = 128
SSM_OCT = 8
VMEM_LIMIT = 56 * 1024 * 1024

HIGHEST = lax.Precision.HIGHEST


def _cparams(*sem):
    return pltpu.CompilerParams(dimension_semantics=sem, vmem_limit_bytes=VMEM_LIMIT)


def _dot(a, b, precision=None):
    return jnp.dot(a, b, preferred_element_type=F32, precision=precision)


def _dot_nt(a, b, precision=None):
    return lax.dot_general(a, b, (((1,), (1,)), ((), ())),
                           preferred_element_type=F32, precision=precision)


def _gelu_tanh(x):
    return 0.5 * x * (1.0 + jnp.tanh(math.sqrt(2.0 / math.pi) * (x + 0.044715 * (x * x * x))))


def _sigmoid(x):
    return 1.0 / (1.0 + jnp.exp(-x))


def _t5_bucket_np(dist):
    n = np.maximum(dist, 0)
    max_exact = N_BUCKETS // 2
    nf = np.maximum(n, 1).astype(np.float32)
    large = max_exact + (np.log(nf / np.float32(max_exact))
                         / np.float32(math.log(MAX_DISTANCE / max_exact))
                         * np.float32(N_BUCKETS - max_exact)).astype(np.int32)
    large = np.minimum(large, N_BUCKETS - 1)
    return np.where(n < max_exact, n, large).astype(np.int32)


def _inproj_kernel(x_ref, g_ref, wu_ref, wc_ref, wk_ref, wqt_ref, wvt_ref, wgt_ref,
                   u_ref, kc_ref, vc_ref, ks_ref, kw_ref, qt_ref, vst_ref, vwt_ref, gt_ref, *, tm):
    x = x_ref[0]
    ms = jnp.mean(x * x, axis=-1, keepdims=True)
    h = (x * lax.rsqrt(ms + EPS) * g_ref[...]).astype(BF16)
    u_ref[...] = _dot(h, wu_ref[...])
    dkv = N_KV_HEADS * HEAD_DIM
    c = _dot(h, wc_ref[...])
    kc_ref[0] = c[:, :dkv]
    vc_ref[0] = c[:, dkv:]
    ka = _dot(h, wk_ref[...])
    blk = (pl.program_id(1) * tm + lax.broadcasted_iota(jnp.int32, (tm, LANES), 0)) // SEL_BLOCK
    onehot = jnp.where(lax.broadcasted_iota(jnp.int32, (tm, LANES), 1) - HEAD_DIM == blk, 1.0, 0.0)
    for hk in range(N_KV_HEADS):
        ks_ref[0, hk] = (ka[:, hk * LANES:(hk + 1) * LANES] + onehot).astype(BF16)
        kw_ref[0, hk] = ka[:, (N_KV_HEADS + hk) * LANES:(N_KV_HEADS + hk + 1) * LANES].astype(BF16)
    qt_ref[0] = (_dot_nt(wqt_ref[...], h) * (HEAD_DIM ** -0.5)).astype(BF16)
    vt = _dot_nt(wvt_ref[...], h).astype(BF16)
    for hk in range(N_KV_HEADS):
        vst_ref[0, hk] = vt[hk * HEAD_DIM:(hk + 1) * HEAD_DIM]
        vwt_ref[0, hk] = vt[dkv + hk * HEAD_DIM:dkv + (hk + 1) * HEAD_DIM]
    gt_ref[0] = _dot_nt(wgt_ref[...], h)


def _inproj(x, g, w_in, tm):
    bsz, seq, d = x.shape
    d_ssm = d // 2
    d_attn = d - d_ssm
    dkv = N_KV_HEADS * HEAD_DIM
    o_kv = d_ssm + d_attn
    o_g = o_kv + 6 * dkv
    assert seq // SEL_BLOCK <= LANES - HEAD_DIM
    wb = w_in.astype(BF16)
    wu = wb[:, :d_ssm]
    wq = wb[:, d_ssm:o_kv]
    k_cmp, v_cmp, k_sel, v_sel, k_win, v_win = [wb[:, o_kv + i * dkv:o_kv + (i + 1) * dkv] for i in range(6)]
    wc = jnp.concatenate([k_cmp, v_cmp], axis=1)

    def pad_heads(w):
        w = w.reshape(d, N_KV_HEADS, HEAD_DIM)
        return jnp.pad(w, ((0, 0), (0, 0), (0, LANES - HEAD_DIM))).reshape(d, N_KV_HEADS * LANES)

    wk = jnp.concatenate([pad_heads(k_sel), pad_heads(k_win)], axis=1)
    wqt = wq.T
    wvt = jnp.concatenate([v_sel, v_win], axis=1).T
    n_gate_rows = 4 * SUBLANES
    wgt = jnp.pad(wb[:, o_g:], ((0, 0), (0, n_gate_rows - 3 * N_HEADS))).T
    full = lambda shape: pl.BlockSpec(shape, lambda b, i: (0,) * len(shape))
    k_spec = pl.BlockSpec((1, N_KV_HEADS, tm, LANES), lambda b, i: (b, 0, i, 0))
    vt_spec = pl.BlockSpec((1, N_KV_HEADS, HEAD_DIM, tm), lambda b, i: (b, 0, 0, i))
    k_shape = jax.ShapeDtypeStruct((bsz, N_KV_HEADS, seq, LANES), BF16)
    vt_shape = jax.ShapeDtypeStruct((bsz, N_KV_HEADS, HEAD_DIM, seq), BF16)
    return pl.pallas_call(
        functools.partial(_inproj_kernel, tm=tm),
        grid=(bsz, seq // tm),
        in_specs=[pl.BlockSpec((1, tm, d), lambda b, i: (b, i, 0)),
                  full((1, d)), full(wu.shape), full(wc.shape), full(wk.shape),
                  full(wqt.shape), full(wvt.shape), full(wgt.shape)],
        out_specs=[pl.BlockSpec((tm, d_ssm), lambda b, i: (i, b)),
                   pl.BlockSpec((1, tm, dkv), lambda b, i: (b, i, 0)),
                   pl.BlockSpec((1, tm, dkv), lambda b, i: (b, i, 0)),
                   k_spec, k_spec,
                   pl.BlockSpec((1, d_attn, tm), lambda b, i: (b, 0, i)),
                   vt_spec, vt_spec,
                   pl.BlockSpec((1, n_gate_rows, tm), lambda b, i: (b, 0, i))],
        out_shape=[jax.ShapeDtypeStruct((seq, bsz * d_ssm), F32),
                   jax.ShapeDtypeStruct((bsz, seq, dkv), F32),
                   jax.ShapeDtypeStruct((bsz, seq, dkv), F32),
                   k_shape, k_shape,
                   jax.ShapeDtypeStruct((bsz, d_attn, seq), BF16),
                   vt_shape, vt_shape,
                   jax.ShapeDtypeStruct((bsz, n_gate_rows, seq), F32)],
        compiler_params=_cparams("parallel", "parallel"),
        name="inproj",
    )(x, g.reshape(1, d), wu, wc, wk, wqt, wvt, wgt)


def _s5_kernel(u_ref, bmat_ref, cmat_ref, lam_ref, d_ref, wglu_ref, y_ref, bu_ref, st_ref,
               *, tc, bsz, n_oct, oct_c, oct_s, lane_chunk):
    rows = tc * bsz
    d_ssm = n_oct * oct_c
    n_state = n_oct * oct_s

    @pl.when(pl.program_id(0) == 0)
    def _():
        st_ref[...] = jnp.zeros_like(st_ref)

    u = u_ref[...].reshape(rows, d_ssm)
    ub = u.astype(BF16)
    for o in range(n_oct):
        r = _dot(ub[:, o * oct_c:(o + 1) * oct_c], bmat_ref[o])
        bu_ref[:, o * oct_s:(o + 1) * oct_s] = r[:, :oct_s]
        bu_ref[:, n_state + o * oct_s:n_state + (o + 1) * oct_s] = r[:, oct_s:]

    for c0 in range(0, n_state, lane_chunk):
        re = slice(c0, c0 + lane_chunk)
        im = slice(n_state + c0, n_state + c0 + lane_chunk)

        def step(t, carry, re=re, im=im):
            sr, si = carry
            row = pl.multiple_of(t * bsz, bsz)
            lr = lam_ref[:, re]
            li = lam_ref[:, im]
            nr = lr * sr - li * si + bu_ref[pl.ds(row, bsz), re]
            ni = lr * si + li * sr + bu_ref[pl.ds(row, bsz), im]
            bu_ref[pl.ds(row, bsz), re] = nr
            bu_ref[pl.ds(row, bsz), im] = ni
            return nr, ni

        sr, si = lax.fori_loop(0, tc, step, (st_ref[:, re], st_ref[:, im]))
        st_ref[:, re] = sr
        st_ref[:, im] = si

    ys = []
    for o in range(n_oct):
        s_re = bu_ref[:, o * oct_s:(o + 1) * oct_s].astype(BF16)
        s_im = bu_ref[:, n_state + o * oct_s:n_state + (o + 1) * oct_s].astype(BF16)
        ys.append(_dot(s_re, cmat_ref[o, :oct_s]) + _dot(s_im, cmat_ref[o, oct_s:]))
    y = jnp.concatenate(ys, axis=1) + d_ref[...] * u
    y = _gelu_tanh(y).astype(BF16)
    z = _dot(y, wglu_ref[...])
    out = z[:, :d_ssm] * _sigmoid(z[:, d_ssm:])
    y_ref[...] = out.astype(BF16).reshape(tc, bsz, d_ssm)


def _s5_params(a_re, a_im, b_re, b_im, c_re, c_im, log_dt, bsz):
    g, p = a_re.shape
    h = b_re.shape[-1]
    n_oct = g // SSM_OCT
    dt = jnp.exp(log_dt)[:, None]
    mag = jnp.exp(a_re * dt)
    lr = mag * jnp.cos(a_im * dt)
    li = mag * jnp.sin(a_im * dt)
    den = a_re * a_re + a_im * a_im
    kr = ((lr - 1.0) * a_re + li * a_im) / den
    ki = (li * a_re - (lr - 1.0) * a_im) / den
    bb_re = kr[..., None] * b_re - ki[..., None] * b_im
    bb_im = kr[..., None] * b_im + ki[..., None] * b_re
    eye = jnp.eye(SSM_OCT, dtype=F32)

    def bd_in(m):
        m = m.reshape(n_oct, SSM_OCT, p, h)
        return jnp.einsum('ogph,gk->oghkp', m, eye).reshape(n_oct, SSM_OCT * h, SSM_OCT * p)

    def bd_out(m):
        m = m.reshape(n_oct, SSM_OCT, h, p)
        return jnp.einsum('oghp,gk->ogpkh', m, eye).reshape(n_oct, SSM_OCT * p, SSM_OCT * h)

    bmat = jnp.concatenate([bd_in(bb_re), bd_in(bb_im)], axis=-1).astype(BF16)
    cmat = jnp.concatenate([bd_out(c_re), bd_out(-c_im)], axis=1).astype(BF16)
    lam = jnp.concatenate([lr.reshape(1, g * p), li.reshape(1, g * p)], axis=1)
    lam = jnp.broadcast_to(lam, (bsz, 2 * g * p))
    return bmat, cmat, lam


def _s5(u_tm, bsz, a_re, a_im, b_re, b_im, c_re, c_im, d, log_dt, w_glu, tc):
    seq = u_tm.shape[0]
    d_ssm = u_tm.shape[1] // bsz
    g, p = a_re.shape
    n_oct = g // SSM_OCT
    oct_c = SSM_OCT * SSM_GROUP
    oct_s = SSM_OCT * p
    bmat, cmat, lam = _s5_params(a_re, a_im, b_re, b_im, c_re, c_im, log_dt, bsz)
    u3 = u_tm.reshape(seq, bsz, d_ssm)
    full = lambda shape: pl.BlockSpec(shape, lambda i: (0,) * len(shape))
    kern = functools.partial(_s5_kernel, tc=tc, bsz=bsz, n_oct=n_oct, oct_c=oct_c, oct_s=oct_s,
                             lane_chunk=min(1024, g * p))
    y = pl.pallas_call(
        kern,
        grid=(seq // tc,),
        in_specs=[pl.BlockSpec((tc, bsz, d_ssm), lambda i: (i, 0, 0)),
                  full(bmat.shape), full(cmat.shape), full(lam.shape),
                  full((1, d_ssm)), full(w_glu.shape)],
        out_specs=pl.BlockSpec((tc, bsz, d_ssm), lambda i: (i, 0, 0)),
        out_shape=jax.ShapeDtypeStruct((seq, bsz, d_ssm), BF16),
        scratch_shapes=[pltpu.VMEM((tc * bsz, 2 * g * p), F32),
                        pltpu.VMEM((bsz, 2 * g * p), F32)],
        compiler_params=_cparams("arbitrary"),
        name="s5_mixer",
    )(u3, bmat, cmat, lam, d.reshape(1, d_ssm), w_glu.astype(BF16))
    return y.reshape(seq, bsz * d_ssm)


def _compress_kernel(k_ref, v_ref, w1a_ref, w1b_ref, posw_ref, w2_ref, w2t_ref, kc_ref, vct_ref):
    def hidden(i, src):
        x = src[0]
        a = _dot(x, w1a_ref[i], HIGHEST)
        b = _dot(x, w1b_ref[i], HIGHEST)
        return _gelu_tanh(a + pltpu.roll(b, a.shape[0] - 1, 0) + posw_ref[i])

    kc = _dot(hidden(0, k_ref), w2_ref[...], HIGHEST).astype(BF16)
    vct = _dot_nt(w2t_ref[...], hidden(1, v_ref), HIGHEST).astype(BF16)
    for hk in range(N_KV_HEADS):
        kc_ref[0, hk] = kc[:, hk * HEAD_DIM:(hk + 1) * HEAD_DIM]
        vct_ref[0, hk] = vct[hk * HEAD_DIM:(hk + 1) * HEAD_DIM]


def _compress(k_cmp, v_cmp, cmp_pos, cmp_w1, cmp_w2):
    bsz, seq, dkv = k_cmp.shape
    n_chunk = seq // CMP_STRIDE
    kview = k_cmp.reshape(bsz, n_chunk, CMP_STRIDE * dkv)
    vview = v_cmp.reshape(bsz, n_chunk, CMP_STRIDE * dkv)
    eye = jnp.eye(N_KV_HEADS, dtype=F32)
    w1 = jnp.einsum('ilde,hk->ilhdke', cmp_w1, eye)
    w1 = w1.reshape(2, CMP_BLOCK, dkv, dkv)
    w1a = w1[:, :CMP_STRIDE].reshape(2, CMP_STRIDE * dkv, dkv)
    w1b = w1[:, CMP_STRIDE:].reshape(2, CMP_STRIDE * dkv, dkv)
    posw = jnp.einsum('ild,ilde->ie', cmp_pos, cmp_w1, precision=HIGHEST)
    posw = jnp.tile(posw, (1, N_KV_HEADS)).reshape(2, 1, dkv)
    w2 = jnp.einsum('ief,hk->ihekf', cmp_w2, eye).reshape(2, dkv, dkv)
    full = lambda shape: pl.BlockSpec(shape, lambda b: (0,) * len(shape))
    return pl.pallas_call(
        _compress_kernel,
        grid=(bsz,),
        in_specs=[pl.BlockSpec((1, n_chunk, CMP_STRIDE * dkv), lambda b: (b, 0, 0)),
                  pl.BlockSpec((1, n_chunk, CMP_STRIDE * dkv), lambda b: (b, 0, 0)),
                  full(w1a.shape), full(w1b.shape), full(posw.shape), full((dkv, dkv)), full((dkv, dkv))],
        out_specs=[pl.BlockSpec((1, N_KV_HEADS, n_chunk, HEAD_DIM), lambda b: (b, 0, 0, 0)),
                   pl.BlockSpec((1, N_KV_HEADS, HEAD_DIM, n_chunk), lambda b: (b, 0, 0, 0))],
        out_shape=[jax.ShapeDtypeStruct((bsz, N_KV_HEADS, n_chunk, HEAD_DIM), BF16),
                   jax.ShapeDtypeStruct((bsz, N_KV_HEADS, HEAD_DIM, n_chunk), BF16)],
        compiler_params=_cparams("parallel"),
        name="compress",
    )(kview, vview, w1a, w1b, posw, w2[0], w2[1].T)


CMP_NEAR_BACK = 2 * SUBLANES
CMP_NEAR_ROWS = 3 * SUBLANES


def _bias_tables(rel_bias, seq):
    last = rel_bias[N_BUCKETS - 1]
    assert int(_t5_bucket_np(np.arange(ATT_TILE + 1, max(seq, 2 * ATT_TILE))).min()) == N_BUCKETS - 1
    s = np.arange(ATT_TILE)[:, None]
    t = np.arange(ATT_TILE)[None, :]

    def table(dist):
        return jnp.transpose(rel_bias[jnp.asarray(_t5_bucket_np(dist))] - last, (2, 0, 1))

    causal = jnp.asarray(t >= s)
    pd = jnp.where(causal, table(t - s), NEG)
    p1 = table(ATT_TILE + t - s)
    w_old = jnp.where(jnp.asarray(s > t), 0.0, NEG).astype(F32)
    m = np.arange(CMP_NEAR_ROWS)[:, None] - CMP_NEAR_BACK
    dist_c = t - (CMP_STRIDE * m + CMP_BLOCK - 1)
    assert dist_c[0].min() > ATT_TILE
    pc = jnp.where(jnp.asarray(dist_c >= 0), table(dist_c), 0.0)
    return pd, p1, w_old, pc


def _cmp_attn_kernel(qt_ref, kc_ref, vct_ref, pc_ref, ovl_ref, ocmpt_ref, negm_ref, s_scr,
                     *, tq, n_sel, n_top):
    qi = pl.program_id(1)
    n_chunk = kc_ref.shape[2]
    n_i = lax.broadcasted_iota(jnp.int32, (n_chunk, tq), 0)
    t_i = qi * tq + lax.broadcasted_iota(jnp.int32, (n_chunk, tq), 1)
    valid = n_i * CMP_STRIDE + (CMP_BLOCK - 1) <= t_i
    jrow = lax.broadcasted_iota(jnp.int32, (n_sel, tq), 0)
    blk_t = (qi * tq + lax.broadcasted_iota(jnp.int32, (n_sel, tq), 1)) // SEL_BLOCK
    forced = (jrow == 0) | ((jrow <= blk_t) & (jrow > blk_t - N_LOCAL_BLOCKS))
    future = jrow > blk_t
    jl = lax.broadcasted_iota(jnp.int32, (SUBLANES, tq), 0)
    s_scr[0:CMP_NEAR_BACK, :] = jnp.zeros((CMP_NEAR_BACK, tq), F32)
    near = pl.ds(pl.multiple_of(qi * (tq // CMP_STRIDE), SUBLANES), CMP_NEAR_ROWS)
    body = slice(CMP_NEAR_BACK, CMP_NEAR_BACK + n_chunk)
    for hk in range(N_KV_HEADS):
        kc = kc_ref[0, hk]
        vct = vct_ref[0, hk]
        psum = jnp.zeros((n_chunk, tq), F32)
        for g in range(GQA):
            hd = hk * GQA + g
            rows = slice(hd * HEAD_DIM, (hd + 1) * HEAD_DIM)
            s_scr[body, :] = _dot(kc, qt_ref[0, rows, :])
            s_scr[near, :] = s_scr[near, :] + pc_ref[hd]
            s = jnp.where(valid, s_scr[body, :], NEG)
            m = jnp.max(s, axis=0, keepdims=True)
            e = jnp.where(valid, jnp.exp(s - m), 0.0)
            l = jnp.sum(e, axis=0, keepdims=True)
            p = e * jnp.where(l > 0.0, 1.0 / l, 0.0)
            psum = psum + p
            ocmpt_ref[0, rows, :] = _dot(vct, p.astype(BF16))
        imp = _dot(ovl_ref[...], psum, HIGHEST)
        v = jnp.where(forced, BIG, jnp.where(future, -BIG, imp))
        n_t = n_sel // SUBLANES
        vt = [v[a * SUBLANES:(a + 1) * SUBLANES] for a in range(n_t)]
        rank = [jnp.zeros((SUBLANES, tq), F32) for _ in range(n_t)]
        for jp in range(n_sel):
            row = jnp.broadcast_to(v[jp:jp + 1, :], (SUBLANES, tq))
            for a in range(n_t):
                if a > jp // SUBLANES:
                    hit = jnp.where(row >= vt[a], 1.0, 0.0)
                elif a < jp // SUBLANES:
                    hit = jnp.where(row > vt[a], 1.0, 0.0)
                else:
                    hit = jnp.where(jl > jp % SUBLANES, jnp.where(row >= vt[a], 1.0, 0.0),
                                    jnp.where(row > vt[a], 1.0, 0.0))
                rank[a] = rank[a] + hit
        rank = jnp.concatenate(rank, axis=0)
        negm_ref[0, hk] = jnp.where(rank < float(n_top), 0.0, NEG).astype(BF16)


def _cmp_attn(qt, kc, vct, pc, tq):
    bsz, d_attn, seq = qt.shape
    n_chunk = kc.shape[2]
    n_sel = seq // SEL_BLOCK
    n_top = min(SEL_TOPK, n_sel)
    n_cmp = (seq - CMP_BLOCK) // CMP_STRIDE + 1
    assert n_sel % SUBLANES == 0 and tq % CMP_STRIDE == 0 and (tq // CMP_STRIDE) % SUBLANES == 0
    cs = np.arange(n_chunk)[:, None] * CMP_STRIDE
    ss = np.arange(n_sel)[None, :] * SEL_BLOCK
    ovl = np.clip(np.minimum(cs + CMP_BLOCK, ss + SEL_BLOCK) - np.maximum(cs, ss), 0, None) / CMP_BLOCK
    ovl[n_cmp:] = 0.0
    ovl_t = jnp.asarray(ovl.T, dtype=F32)
    kern = functools.partial(_cmp_attn_kernel, tq=tq, n_sel=n_sel, n_top=n_top)
    return pl.pallas_call(
        kern,
        grid=(bsz, seq // tq),
        in_specs=[pl.BlockSpec((1, d_attn, tq), lambda b, i: (b, 0, i)),
                  pl.BlockSpec((1, N_KV_HEADS, n_chunk, HEAD_DIM), lambda b, i: (b, 0, 0, 0)),
                  pl.BlockSpec((1, N_KV_HEADS, HEAD_DIM, n_chunk), lambda b, i: (b, 0, 0, 0)),
                  pl.BlockSpec(pc.shape, lambda b, i: (0, 0, 0)),
                  pl.BlockSpec((n_sel, n_chunk), lambda b, i: (0, 0))],
        out_specs=[pl.BlockSpec((1, d_attn, tq), lambda b, i: (b, 0, i)),
                   pl.BlockSpec((1, N_KV_HEADS, n_sel, tq), lambda b, i: (b, 0, 0, i))],
        out_shape=[jax.ShapeDtypeStruct((bsz, d_attn, seq), F32),
                   jax.ShapeDtypeStruct((bsz, N_KV_HEADS, n_sel, seq), BF16)],
        scratch_shapes=[pltpu.VMEM((CMP_NEAR_BACK + n_chunk, tq), F32)],
        compiler_params=_cparams("parallel", "parallel"),
        name="cmp_attn",
    )(qt, kc, vct, pc, ovl_t)


def _sw_attn_kernel(qt_ref, negm_ref, ks_ref, vst_ref, kw_ref, vwt_ref, ocmpt_ref, gt_ref,
                    pd_ref, p1_ref, wold_ref, eye_ref, o_ref, qa_ref, acc_s, acc_w, *, tq, n_win_tiles):
    qi = pl.program_id(1)
    gates = _sigmoid(gt_ref[0])

    def span(kt, n=1):
        return pl.ds(pl.multiple_of(kt * (n * ATT_TILE), n * ATT_TILE), n * ATT_TILE)

    def col_max(s):
        return jnp.max(s, axis=0, keepdims=True)

    def col_sum(p):
        return jnp.sum(p, axis=0, keepdims=True)

    def softmax_first(s_list, v_list, acc, hk):
        m = functools.reduce(jnp.maximum, [col_max(s) for s in s_list])
        ps = [jnp.exp(s - m) for s in s_list]
        acc[hk] = sum(_dot(v, p.astype(BF16)) for v, p in zip(v_list, ps))
        return m, sum(col_sum(p) for p in ps)

    def softmax_update(m, l, s, v, acc, hk):
        m_new = jnp.maximum(m, col_max(s))
        alpha = jnp.exp(m - m_new)
        p = jnp.exp(s - m_new)
        acc[hk] = alpha * acc[hk] + _dot(v, p.astype(BF16))
        return m_new, alpha * l + col_sum(p)

    def exists(back):
        return jnp.where(qi >= back, 0.0, NEG)

    w_old = jnp.concatenate([wold_ref[...]] * GQA, axis=1)
    n_far = jnp.maximum(qi - 1, 0)
    state = []
    o_win = []
    for hk in range(N_KV_HEADS):
        heads = range(hk * GQA, (hk + 1) * GQA)
        q4 = jnp.concatenate([qt_ref[0, hd * HEAD_DIM:(hd + 1) * HEAD_DIM, :] for hd in heads], axis=1)
        nm = negm_ref[0, hk]
        if nm.shape[0] < LANES - HEAD_DIM:
            nm = jnp.concatenate([nm, jnp.zeros((LANES - HEAD_DIM - nm.shape[0], tq), BF16)], axis=0)
        qa = jnp.concatenate([q4, jnp.concatenate([nm] * GQA, axis=1)], axis=0)
        qa_ref[hk] = qa
        qw = jnp.concatenate([q4, jnp.zeros_like(q4)], axis=0)
        pd4 = jnp.concatenate([pd_ref[hd] for hd in heads], axis=1)
        p14 = jnp.concatenate([p1_ref[hd] for hd in heads], axis=1)

        kt1 = jnp.maximum(qi - 1, 0)
        s_near = [_dot(ks_ref[0, hk, span(qi), :], qa) + pd4,
                  _dot(ks_ref[0, hk, span(kt1), :], qa) + (p14 + exists(1))]
        v_near = [vst_ref[0, hk, :, span(qi)], vst_ref[0, hk, :, span(kt1)]]
        state += list(softmax_first(s_near, v_near, acc_s, hk))

        s_win = [_dot(kw_ref[0, hk, span(qi), :], qw) + pd4]
        v_win = [vwt_ref[0, hk, :, span(qi)]]
        for back in range(1, n_win_tiles + 1):
            kt = jnp.maximum(qi - back, 0)
            s = _dot(kw_ref[0, hk, span(kt), :], qw)
            if back == 1:
                s = s + (p14 + exists(back))
            elif back == n_win_tiles:
                s = s + (w_old + exists(back))
            else:
                s = s + exists(back)
            s_win.append(s)
            v_win.append(vwt_ref[0, hk, :, span(kt)])
        _, l_w = softmax_first(s_win, v_win, acc_w, hk)
        o_win.append(acc_w[hk] * (1.0 / l_w))

    def far(j, carry):
        out = []
        for hk in range(N_KV_HEADS):
            s = _dot(ks_ref[0, hk, span(j, 2), :], qa_ref[hk])
            out += list(softmax_update(carry[2 * hk], carry[2 * hk + 1], s, vst_ref[0, hk, :, span(j, 2)],
                                       acc_s, hk))
        return tuple(out)

    state = lax.fori_loop(0, n_far // 2, far, tuple(state))
    k_odd = jnp.maximum(n_far - 1, 0)
    odd = jnp.where(n_far % 2 == 1, 0.0, NEG)

    ys = []
    for hk in range(N_KV_HEADS):
        s = _dot(ks_ref[0, hk, span(k_odd), :], qa_ref[hk]) + odd
        _, l_s = softmax_update(state[2 * hk], state[2 * hk + 1], s, vst_ref[0, hk, :, span(k_odd)], acc_s, hk)
        o_sel = acc_s[hk] * (1.0 / l_s)
        for g in range(GQA):
            hd = hk * GQA + g
            cols = slice(g * tq, (g + 1) * tq)
            ys.append(gates[3 * hd:3 * hd + 1] * ocmpt_ref[0, hd * HEAD_DIM:(hd + 1) * HEAD_DIM, :]
                      + gates[3 * hd + 1:3 * hd + 2] * o_sel[:, cols]
                      + gates[3 * hd + 2:3 * hd + 3] * o_win[hk][:, cols])
    yt = jnp.concatenate(ys, axis=0).astype(BF16)
    o_ref[0] = _dot_nt(eye_ref[...], yt).astype(BF16)


def _sw_attn(qt, negm, ks, vst, kw, vwt, ocmpt, gt, pd, p1, w_old, tq):
    bsz, d_attn, seq = qt.shape
    n_sel = negm.shape[2]
    assert tq == ATT_TILE and WINDOW % ATT_TILE == 0 and n_sel <= LANES - HEAD_DIM
    eye = jnp.eye(tq, dtype=BF16)
    kern = functools.partial(_sw_attn_kernel, tq=tq, n_win_tiles=WINDOW // ATT_TILE)
    k_spec = pl.BlockSpec((1, N_KV_HEADS, seq, LANES), lambda b, i: (b, 0, 0, 0))
    vt_spec = pl.BlockSpec((1, N_KV_HEADS, HEAD_DIM, seq), lambda b, i: (b, 0, 0, 0))
    qt_spec = pl.BlockSpec((1, d_attn, tq), lambda b, i: (b, 0, i))
    full = lambda shape: pl.BlockSpec(shape, lambda b, i: (0,) * len(shape))
    return pl.pallas_call(
        kern,
        grid=(bsz, seq // tq),
        in_specs=[qt_spec,
                  pl.BlockSpec((1, N_KV_HEADS, n_sel, tq), lambda b, i: (b, 0, 0, i)),
                  k_spec, vt_spec, k_spec, vt_spec,
                  qt_spec,
                  pl.BlockSpec((1, gt.shape[1], tq), lambda b, i: (b, 0, i)),
                  full(pd.shape), full(p1.shape), full(w_old.shape), full(eye.shape)],
        out_specs=pl.BlockSpec((1, tq, d_attn), lambda b, i: (b, i, 0)),
        out_shape=jax.ShapeDtypeStruct((bsz, seq, d_attn), BF16),
        scratch_shapes=[pltpu.VMEM((N_KV_HEADS, LANES, GQA * tq), BF16),
                        pltpu.VMEM((N_KV_HEADS, HEAD_DIM, GQA * tq), F32),
                        pltpu.VMEM((N_KV_HEADS, HEAD_DIM, GQA * tq), F32)],
        compiler_params=_cparams("parallel", "parallel"),
        name="sel_win_attn",
    )(qt, negm, ks, vst, kw, vwt, ocmpt, gt, pd, p1, w_old, eye)


def _outproj_kernel(x_ref, ys_ref, ya_ref, ws_ref, wa_ref, o_ref):
    o_ref[0] = x_ref[0] + _dot(ys_ref[...], ws_ref[...]) + _dot(ya_ref[0], wa_ref[...])


def _outproj(x, y_ssm_tm, y_attn, w_out, tm):
    bsz, seq, d = x.shape
    d_ssm = y_ssm_tm.shape[1] // bsz
    d_attn = y_attn.shape[2]
    wb = w_out.astype(BF16)
    ws, wa = wb[:d_ssm], wb[d_ssm:]
    full = lambda shape: pl.BlockSpec(shape, lambda b, i: (0,) * len(shape))
    return pl.pallas_call(
        _outproj_kernel,
        grid=(bsz, seq // tm),
        in_specs=[pl.BlockSpec((1, tm, d), lambda b, i: (b, i, 0)),
                  pl.BlockSpec((tm, d_ssm), lambda b, i: (i, b)),
                  pl.BlockSpec((1, tm, d_attn), lambda b, i: (b, i, 0)),
                  full(ws.shape), full(wa.shape)],
        out_specs=pl.BlockSpec((1, tm, d), lambda b, i: (b, i, 0)),
        out_shape=jax.ShapeDtypeStruct((bsz, seq, d), F32),
        compiler_params=_cparams("parallel", "parallel"),
        name="outproj",
    )(x, y_ssm_tm, y_attn, ws, wa)


def _router_kernel(x_ref, g_ref, wr_ref, br_ref, h_ref, coef_ref):
    x = x_ref[...]
    ms = jnp.mean(x * x, axis=-1, keepdims=True)
    h = x * lax.rsqrt(ms + EPS) * g_ref[...]
    h_ref[...] = h.astype(BF16)
    logits = _dot(h, wr_ref[...], HIGHEST) + br_ref[...]
    lane = lax.broadcasted_iota(jnp.int32, logits.shape, 1)
    is_grp = (lane >= N_EXPERTS) & (lane < N_EXPERTS + N_GROUPS)
    lg = jnp.where(is_grp, logits, NEG)
    mg = jnp.max(lg, axis=-1, keepdims=True)
    g_val = 1.0 / jnp.sum(jnp.where(is_grp, jnp.exp(lg - mg), 0.0), axis=-1, keepdims=True)
    g_idx = jnp.min(jnp.where(is_grp & (lg == mg), lane - N_EXPERTS, N_GROUPS), axis=-1, keepdims=True)
    in_grp = (lane >= g_idx * EXPERTS_PER_GROUP) & (lane < (g_idx + 1) * EXPERTS_PER_GROUP)
    le = jnp.where(in_grp, logits, NEG)
    me = jnp.max(le, axis=-1, keepdims=True)
    ee = jnp.where(in_grp, jnp.exp(le - me), 0.0)
    pe = ee / jnp.sum(ee, axis=-1, keepdims=True)
    p1 = jnp.max(pe, axis=-1, keepdims=True)
    i1 = jnp.min(jnp.where(in_grp & (pe == p1), lane, LANES), axis=-1, keepdims=True)
    rest = in_grp & (lane != i1)
    pr = jnp.where(rest, pe, -1.0)
    p2 = jnp.max(pr, axis=-1, keepdims=True)
    i2 = jnp.min(jnp.where(rest & (pr == p2), lane, LANES), axis=-1, keepdims=True)
    tot = p1 + p2
    coef_ref[...] = (jnp.where(lane == i1, g_val * (p1 / tot), 0.0)
                     + jnp.where(lane == i2, g_val * (p2 / tot), 0.0))


def _router(x2d, g, rg_w, rg_b, re_w, re_b, tm):
    tok, d = x2d.shape
    we = jnp.transpose(re_w, (1, 0, 2)).reshape(d, N_EXPERTS)
    wr = jnp.pad(jnp.concatenate([we, rg_w], axis=1), ((0, 0), (0, LANES - N_EXPERTS - N_GROUPS)))
    br = jnp.pad(jnp.concatenate([re_b.reshape(-1), rg_b]), (0, LANES - N_EXPERTS - N_GROUPS)).reshape(1, LANES)
    full = lambda shape: pl.BlockSpec(shape, lambda i: (0,) * len(shape))
    return pl.pallas_call(
        _router_kernel,
        grid=(tok // tm,),
        in_specs=[pl.BlockSpec((tm, d), lambda i: (i, 0)), full((1, d)), full(wr.shape), full(br.shape)],
        out_specs=[pl.BlockSpec((tm, d), lambda i: (i, 0)), pl.BlockSpec((tm, LANES), lambda i: (i, 0))],
        out_shape=[jax.ShapeDtypeStruct((tok, d), BF16), jax.ShapeDtypeStruct((tok, LANES), F32)],
        compiler_params=_cparams("parallel"),
        name="router",
    )(x2d, g.reshape(1, d), wr, br)


def _moe_kernel(h_ref, coef_ref, x_ref, wg_ref, wu_ref, wd_ref, o_ref):
    e = pl.program_id(1)

    @pl.when(e == 0)
    def _():
        o_ref[...] = x_ref[...]

    h = h_ref[...]
    coef = coef_ref[...]
    lane = lax.broadcasted_iota(jnp.int32, coef.shape, 1)
    c = jnp.sum(jnp.where(lane == e, coef, 0.0), axis=-1, keepdims=True)
    a = _dot(h, wg_ref[0])
    hid = (a * _sigmoid(a)) * _dot(h, wu_ref[0]) * c
    o_ref[...] += _dot(hid.astype(BF16), wd_ref[0])


def _moe(h, coef, x2d, w_gate, w_up, w_down, tm):
    tok, d = x2d.shape
    f = w_gate.shape[-1]
    wg = w_gate.reshape(N_EXPERTS, d, f).astype(BF16)
    wu = w_up.reshape(N_EXPERTS, d, f).astype(BF16)
    wd = w_down.reshape(N_EXPERTS, f, d).astype(BF16)
    return pl.pallas_call(
        _moe_kernel,
        grid=(tok // tm, N_EXPERTS),
        in_specs=[pl.BlockSpec((tm, d), lambda i, e: (i, 0)),
                  pl.BlockSpec((tm, LANES), lambda i, e: (i, 0)),
                  pl.BlockSpec((tm, d), lambda i, e: (i, 0)),
                  pl.BlockSpec((1, d, f), lambda i, e: (e, 0, 0)),
                  pl.BlockSpec((1, d, f), lambda i, e: (e, 0, 0)),
                  pl.BlockSpec((1, f, d), lambda i, e: (e, 0, 0))],
        out_specs=pl.BlockSpec((tm, d), lambda i, e: (i, 0)),
        out_shape=jax.ShapeDtypeStruct((tok, d), F32),
        compiler_params=_cparams("parallel", "arbitrary"),
        name="moe_experts",
    )(h, coef, x2d, wg, wu, wd)


def _norm_kernel(x_ref, g_ref, o_ref):
    x = x_ref[...]
    ms = jnp.mean(x * x, axis=-1, keepdims=True)
    o_ref[...] = x * lax.rsqrt(ms + EPS) * g_ref[...]


def _final_norm(x2d, g, tm):
    tok, d = x2d.shape
    return pl.pallas_call(
        _norm_kernel,
        grid=(tok // tm,),
        in_specs=[pl.BlockSpec((tm, d), lambda i: (i, 0)), pl.BlockSpec((1, d), lambda i: (0, 0))],
        out_specs=pl.BlockSpec((tm, d), lambda i: (i, 0)),
        out_shape=jax.ShapeDtypeStruct((tok, d), F32),
        compiler_params=_cparams("parallel"),
        name="final_norm",
    )(x2d, g.reshape(1, d))


def kernel(x, norm1_g, norm2_g, final_g, w_in, w_out, ssm_a_re, ssm_a_im, ssm_b_re, ssm_b_im, ssm_c_re, ssm_c_im, ssm_d, ssm_log_dt, ssm_w_glu, cmp_pos, cmp_w1, cmp_w2, rel_bias, router_g_w, router_g_b, router_e_w, router_e_b, exp_w_gate, exp_w_up, exp_w_down):
    bsz, seq, d = x.shape
    depth = norm1_g.shape[0]
    tm = min(512, seq)
    tok_tile = min(1024, bsz * seq)
    pd, p1, w_old, pc = _bias_tables(rel_bias, seq)
    for l in range(depth):
        u_tm, k_cmp, v_cmp, ks, kw, qt, vst, vwt, gt = _inproj(x, norm1_g[l], w_in[l], tm)
        y_ssm = _s5(u_tm, bsz, ssm_a_re[l], ssm_a_im[l], ssm_b_re[l], ssm_b_im[l], ssm_c_re[l],
                    ssm_c_im[l], ssm_d[l].reshape(-1), ssm_log_dt[l], ssm_w_glu[l], tc=min(32, seq))
        kc, vct = _compress(k_cmp, v_cmp, cmp_pos[l], cmp_w1[l], cmp_w2[l])
        ocmpt, negm = _cmp_attn(qt, kc, vct, pc, ATT_TILE)
        y_attn = _sw_attn(qt, negm, ks, vst, kw, vwt, ocmpt, gt, pd, p1, w_old, ATT_TILE)
        x = _outproj(x, y_ssm, y_attn, w_out[l], tm)
        x2d = x.reshape(bsz * seq, d)
        h, coef = _router(x2d, norm2_g[l], router_g_w[l], router_g_b[l], router_e_w[l], router_e_b[l], tok_tile)
        x = _moe(h, coef, x2d, exp_w_gate[l], exp_w_up[l], exp_w_down[l], tok_tile).reshape(bsz, seq, d)
    return _final_norm(x.reshape(bsz * seq, d), final_g, tok_tile).reshape(bsz, seq, d)
```

```python
import functools
import math

import numpy as np
import jax
import jax.numpy as jnp
from jax import lax
from jax.experimental import pallas as pl
from jax.experimental.pallas import tpu as pltpu

F32 = jnp.float32
BF16 = jnp.bfloat16

SSM_GROUP = 16
SSM_STATE = 64
HEAD_DIM = 64
N_KV_HEADS = 2
GQA = 4
N_HEADS = N_KV_HEADS * GQA
CMP_BLOCK = 32
CMP_STRIDE = 16
SEL_BLOCK = 64
SEL_TOPK = 16
N_LOCAL_BLOCKS = 2
WINDOW = 512
N_BUCKETS = 32
MAX_DISTANCE = 128
N_GROUPS = 4
EXPERTS_PER_GROUP = 8
N_EXPERTS = N_GROUPS * EXPERTS_PER_GROUP
EPS = 1e-6
NEG = -1e30
BIG = 1e9

LANES = 128
SUBLANES = 8
ATT_TILE = 128
SSM_OCT = 8
VMEM_LIMIT = 56 * 1024 * 1024

HIGHEST = lax.Precision.HIGHEST


def _cparams(*sem):
    return pltpu.CompilerParams(dimension_semantics=sem, vmem_limit_bytes=VMEM_LIMIT)


def _dot(a, b, precision=None):
    return jnp.dot(a, b, preferred_element_type=F32, precision=precision)


def _dot_nt(a, b, precision=None):
    return lax.dot_general(a, b, (((1,), (1,)), ((), ())),
                           preferred_element_type=F32, precision=precision)


def _gelu_tanh(x):
    return 0.5 * x * (1.0 + jnp.tanh(math.sqrt(2.0 / math.pi) * (x + 0.044715 * (x * x * x))))


def _sigmoid(x):
    return 1.0 / (1.0 + jnp.exp(-x))


def _t5_bucket_np(dist):
    n = np.maximum(dist, 0)
    max_exact = N_BUCKETS // 2
    nf = np.maximum(n, 1).astype(np.float32)
    large = max_exact + (np.log(nf / np.float32(max_exact))
                         / np.float32(math.log(MAX_DISTANCE / max_exact))
                         * np.float32(N_BUCKETS - max_exact)).astype(np.int32)
    large = np.minimum(large, N_BUCKETS - 1)
    return np.where(n < max_exact, n, large).astype(np.int32)


def _inproj_kernel(x_ref, g_ref, wu_ref, wc_ref, wk_ref, wqt_ref, wvt_ref, wgt_ref,
                   u_ref, kc_ref, vc_ref, ks_ref, kw_ref, qt_ref, vst_ref, vwt_ref, gt_ref, *, tm):
    x = x_ref[0]
    ms = jnp.mean(x * x, axis=-1, keepdims=True)
    h = (x * lax.rsqrt(ms + EPS) * g_ref[...]).astype(BF16)
    u_ref[...] = _dot(h, wu_ref[...])
    dkv = N_KV_HEADS * HEAD_DIM
    c = _dot(h, wc_ref[...])
    kc_ref[0] = c[:, :dkv]
    vc_ref[0] = c[:, dkv:]
    ka = _dot(h, wk_ref[...])
    blk = (pl.program_id(1) * tm + lax.broadcasted_iota(jnp.int32, (tm, LANES), 0)) // SEL_BLOCK
    onehot = jnp.where(lax.broadcasted_iota(jnp.int32, (tm, LANES), 1) - HEAD_DIM == blk, 1.0, 0.0)
    for hk in range(N_KV_HEADS):
        ks_ref[0, hk] = (ka[:, hk * LANES:(hk + 1) * LANES] + onehot).astype(BF16)
        kw_ref[0, hk] = ka[:, (N_KV_HEADS + hk) * LANES:(N_KV_HEADS + hk + 1) * LANES].astype(BF16)
    qt_ref[0] = (_dot_nt(wqt_ref[...], h) * (HEAD_DIM ** -0.5)).astype(BF16)
    vt = _dot_nt(wvt_ref[...], h).astype(BF16)
    for hk in range(N_KV_HEADS):
        vst_ref[0, hk] = vt[hk * HEAD_DIM:(hk + 1) * HEAD_DIM]
        vwt_ref[0, hk] = vt[dkv + hk * HEAD_DIM:dkv + (hk + 1) * HEAD_DIM]
    gt_ref[0] = _dot_nt(wgt_ref[...], h)


def _inproj(x, g, w_in, tm):
    bsz, seq, d = x.shape
    d_ssm = d // 2
    d_attn = d - d_ssm
    dkv = N_KV_HEADS * HEAD_DIM
    o_kv = d_ssm + d_attn
    o_g = o_kv + 6 * dkv
    assert seq // SEL_BLOCK <= LANES - HEAD_DIM
    wb = w_in.astype(BF16)
    wu = wb[:, :d_ssm]
    wq = wb[:, d_ssm:o_kv]
    k_cmp, v_cmp, k_sel, v_sel, k_win, v_win = [wb[:, o_kv + i * dkv:o_kv + (i + 1) * dkv] for i in range(6)]
    wc = jnp.concatenate([k_cmp, v_cmp], axis=1)

    def pad_heads(w):
        w = w.reshape(d, N_KV_HEADS, HEAD_DIM)
        return jnp.pad(w, ((0, 0), (0, 0), (0, LANES - HEAD_DIM))).reshape(d, N_KV_HEADS * LANES)

    wk = jnp.concatenate([pad_heads(k_sel), pad_heads(k_win)], axis=1)
    wqt = wq.T
    wvt = jnp.concatenate([v_sel, v_win], axis=1).T
    n_gate_rows = 4 * SUBLANES
    wgt = jnp.pad(wb[:, o_g:], ((0, 0), (0, n_gate_rows - 3 * N_HEADS))).T
    full = lambda shape: pl.BlockSpec(shape, lambda b, i: (0,) * len(shape))
    k_spec = pl.BlockSpec((1, N_KV_HEADS, tm, LANES), lambda b, i: (b, 0, i, 0))
    vt_spec = pl.BlockSpec((1, N_KV_HEADS, HEAD_DIM, tm), lambda b, i: (b, 0, 0, i))
    k_shape = jax.ShapeDtypeStruct((bsz, N_KV_HEADS, seq, LANES), BF16)
    vt_shape = jax.ShapeDtypeStruct((bsz, N_KV_HEADS, HEAD_DIM, seq), BF16)
    return pl.pallas_call(
        functools.partial(_inproj_kernel, tm=tm),
        grid=(bsz, seq // tm),
        in_specs=[pl.BlockSpec((1, tm, d), lambda b, i: (b, i, 0)),
                  full((1, d)), full(wu.shape), full(wc.shape), full(wk.shape),
                  full(wqt.shape), full(wvt.shape), full(wgt.shape)],
        out_specs=[pl.BlockSpec((tm, d_ssm), lambda b, i: (i, b)),
                   pl.BlockSpec((1, tm, dkv), lambda b, i: (b, i, 0)),
                   pl.BlockSpec((1, tm, dkv), lambda b, i: (b, i, 0)),
                   k_spec, k_spec,
                   pl.BlockSpec((1, d_attn, tm), lambda b, i: (b, 0, i)),
                   vt_spec, vt_spec,
                   pl.BlockSpec((1, n_gate_rows, tm), lambda b, i: (b, 0, i))],
        out_shape=[jax.ShapeDtypeStruct((seq, bsz * d_ssm), F32),
                   jax.ShapeDtypeStruct((bsz, seq, dkv), F32),
                   jax.ShapeDtypeStruct((bsz, seq, dkv), F32),
                   k_shape, k_shape,
                   jax.ShapeDtypeStruct((bsz, d_attn, seq), BF16),
                   vt_shape, vt_shape,
                   jax.ShapeDtypeStruct((bsz, n_gate_rows, seq), F32)],
        compiler_params=_cparams("parallel", "parallel"),
        name="inproj",
    )(x, g.reshape(1, d), wu, wc, wk, wqt, wvt, wgt)


def _s5_kernel(u_ref, bmat_ref, cmat_ref, lam_ref, d_ref, wglu_ref, y_ref, bu_ref, st_ref,
               *, tc, bsz, n_oct, oct_c, oct_s, lane_chunk):
    rows = tc * bsz
    d_ssm = n_oct * oct_c
    n_state = n_oct * oct_s

    @pl.when(pl.program_id(0) == 0)
    def _():
        st_ref[...] = jnp.zeros_like(st_ref)

    u = u_ref[...].reshape(rows, d_ssm)
    ub = u.astype(BF16)
    for o in range(n_oct):
        r = _dot(ub[:, o * oct_c:(o + 1) * oct_c], bmat_ref[o])
        bu_ref[:, o * oct_s:(o + 1) * oct_s] = r[:, :oct_s]
        bu_ref[:, n_state + o * oct_s:n_state + (o + 1) * oct_s] = r[:, oct_s:]

    for c0 in range(0, n_state, lane_chunk):
        re = slice(c0, c0 + lane_chunk)
        im = slice(n_state + c0, n_state + c0 + lane_chunk)

        def step(t, carry, re=re, im=im):
            sr, si = carry
            row = pl.multiple_of(t * bsz, bsz)
            lr = lam_ref[:, re]
            li = lam_ref[:, im]
            nr = lr * sr - li * si + bu_ref[pl.ds(row, bsz), re]
            ni = lr * si + li * sr + bu_ref[pl.ds(row, bsz), im]
            bu_ref[pl.ds(row, bsz), re] = nr
            bu_ref[pl.ds(row, bsz), im] = ni
            return nr, ni

        sr, si = lax.fori_loop(0, tc, step, (st_ref[:, re], st_ref[:, im]))
        st_ref[:, re] = sr
        st_ref[:, im] = si

    ys = []
    for o in range(n_oct):
        s_re = bu_ref[:, o * oct_s:(o + 1) * oct_s].astype(BF16)
        s_im = bu_ref[:, n_state + o * oct_s:n_state + (o + 1) * oct_s].astype(BF16)
        ys.append(_dot(s_re, cmat_ref[o, :oct_s]) + _dot(s_im, cmat_ref[o, oct_s:]))
    y = jnp.concatenate(ys, axis=1) + d_ref[...] * u
    y = _gelu_tanh(y).astype(BF16)
    z = _dot(y, wglu_ref[...])
    out = z[:, :d_ssm] * _sigmoid(z[:, d_ssm:])
    y_ref[...] = out.astype(BF16).reshape(tc, bsz, d_ssm)


def _s5_params(a_re, a_im, b_re, b_im, c_re, c_im, log_dt, bsz):
    g, p = a_re.shape
    h = b_re.shape[-1]
    n_oct = g // SSM_OCT
    dt = jnp.exp(log_dt)[:, None]
    mag = jnp.exp(a_re * dt)
    lr = mag * jnp.cos(a_im * dt)
    li = mag * jnp.sin(a_im * dt)
    den = a_re * a_re + a_im * a_im
    kr = ((lr - 1.0) * a_re + li * a_im) / den
    ki = (li * a_re - (lr - 1.0) * a_im) / den
    bb_re = kr[..., None] * b_re - ki[..., None] * b_im
    bb_im = kr[..., None] * b_im + ki[..., None] * b_re
    eye = jnp.eye(SSM_OCT, dtype=F32)

    def bd_in(m):
        m = m.reshape(n_oct, SSM_OCT, p, h)
        return jnp.einsum('ogph,gk->oghkp', m, eye).reshape(n_oct, SSM_OCT * h, SSM_OCT * p)

    def bd_out(m):
        m = m.reshape(n_oct, SSM_OCT, h, p)
        return jnp.einsum('oghp,gk->ogpkh', m, eye).reshape(n_oct, SSM_OCT * p, SSM_OCT * h)

    bmat = jnp.concatenate([bd_in(bb_re), bd_in(bb_im)], axis=-1).astype(BF16)
    cmat = jnp.concatenate([bd_out(c_re), bd_out(-c_im)], axis=1).astype(BF16)
    lam = jnp.concatenate([lr.reshape(1, g * p), li.reshape(1, g * p)], axis=1)
    lam = jnp.broadcast_to(lam, (bsz, 2 * g * p))
    return bmat, cmat, lam


def _s5(u_tm, bsz, a_re, a_im, b_re, b_im, c_re, c_im, d, log_dt, w_glu, tc):
    seq = u_tm.shape[0]
    d_ssm = u_tm.shape[1] // bsz
    g, p = a_re.shape
    n_oct = g // SSM_OCT
    oct_c = SSM_OCT * SSM_GROUP
    oct_s = SSM_OCT * p
    bmat, cmat, lam = _s5_params(a_re, a_im, b_re, b_im, c_re, c_im, log_dt, bsz)
    u3 = u_tm.reshape(seq, bsz, d_ssm)
    full = lambda shape: pl.BlockSpec(shape, lambda i: (0,) * len(shape))
    kern = functools.partial(_s5_kernel, tc=tc, bsz=bsz, n_oct=n_oct, oct_c=oct_c, oct_s=oct_s,
                             lane_chunk=min(1024, g * p))
    y = pl.pallas_call(
        kern,
        grid=(seq // tc,),
        in_specs=[pl.BlockSpec((tc, bsz, d_ssm), lambda i: (i, 0, 0)),
                  full(bmat.shape), full(cmat.shape), full(lam.shape),
                  full((1, d_ssm)), full(w_glu.shape)],
        out_specs=pl.BlockSpec((tc, bsz, d_ssm), lambda i: (i, 0, 0)),
        out_shape=jax.ShapeDtypeStruct((seq, bsz, d_ssm), BF16),
        scratch_shapes=[pltpu.VMEM((tc * bsz, 2 * g * p), F32),
                        pltpu.VMEM((bsz, 2 * g * p), F32)],
        compiler_params=_cparams("arbitrary"),
        name="s5_mixer",
    )(u3, bmat, cmat, lam, d.reshape(1, d_ssm), w_glu.astype(BF16))
    return y.reshape(seq, bsz * d_ssm)


def _compress_kernel(k_ref, v_ref, w1a_ref, w1b_ref, posw_ref, w2_ref, w2t_ref, kc_ref, vct_ref):
    def hidden(i, src):
        x = src[0]
        a = _dot(x, w1a_ref[i], HIGHEST)
        b = _dot(x, w1b_ref[i], HIGHEST)
        return _gelu_tanh(a + pltpu.roll(b, a.shape[0] - 1, 0) + posw_ref[i])

    kc = _dot(hidden(0, k_ref), w2_ref[...], HIGHEST).astype(BF16)
    vct = _dot_nt(w2t_ref[...], hidden(1, v_ref), HIGHEST).astype(BF16)
    for hk in range(N_KV_HEADS):
        kc_ref[0, hk] = kc[:, hk * HEAD_DIM:(hk + 1) * HEAD_DIM]
        vct_ref[0, hk] = vct[hk * HEAD_DIM:(hk + 1) * HEAD_DIM]


def _compress(k_cmp, v_cmp, cmp_pos, cmp_w1, cmp_w2):
    bsz, seq, dkv = k_cmp.shape
    n_chunk = seq // CMP_STRIDE
    kview = k_cmp.reshape(bsz, n_chunk, CMP_STRIDE * dkv)
    vview = v_cmp.reshape(bsz, n_chunk, CMP_STRIDE * dkv)
    eye = jnp.eye(N_KV_HEADS, dtype=F32)
    w1 = jnp.einsum('ilde,hk->ilhdke', cmp_w1, eye)
    w1 = w1.reshape(2, CMP_BLOCK, dkv, dkv)
    w1a = w1[:, :CMP_STRIDE].reshape(2, CMP_STRIDE * dkv, dkv)
    w1b = w1[:, CMP_STRIDE:].reshape(2, CMP_STRIDE * dkv, dkv)
    posw = jnp.einsum('ild,ilde->ie', cmp_pos, cmp_w1, precision=HIGHEST)
    posw = jnp.tile(posw, (1, N_KV_HEADS)).reshape(2, 1, dkv)
    w2 = jnp.einsum('ief,hk->ihekf', cmp_w2, eye).reshape(2, dkv, dkv)
    full = lambda shape: pl.BlockSpec(shape, lambda b: (0,) * len(shape))
    return pl.pallas_call(
        _compress_kernel,
        grid=(bsz,),
        in_specs=[pl.BlockSpec((1, n_chunk, CMP_STRIDE * dkv), lambda b: (b, 0, 0)),
                  pl.BlockSpec((1, n_chunk, CMP_STRIDE * dkv), lambda b: (b, 0, 0)),
                  full(w1a.shape), full(w1b.shape), full(posw.shape), full((dkv, dkv)), full((dkv, dkv))],
        out_specs=[pl.BlockSpec((1, N_KV_HEADS, n_chunk, HEAD_DIM), lambda b: (b, 0, 0, 0)),
                   pl.BlockSpec((1, N_KV_HEADS, HEAD_DIM, n_chunk), lambda b: (b, 0, 0, 0))],
        out_shape=[jax.ShapeDtypeStruct((bsz, N_KV_HEADS, n_chunk, HEAD_DIM), BF16),
                   jax.ShapeDtypeStruct((bsz, N_KV_HEADS, HEAD_DIM, n_chunk), BF16)],
        compiler_params=_cparams("parallel"),
        name="compress",
    )(kview, vview, w1a, w1b, posw, w2[0], w2[1].T)


CMP_NEAR_BACK = 2 * SUBLANES
CMP_NEAR_ROWS = 3 * SUBLANES


def _bias_tables(rel_bias, seq):
    last = rel_bias[N_BUCKETS - 1]
    assert int(_t5_bucket_np(np.arange(ATT_TILE + 1, max(seq, 2 * ATT_TILE))).min()) == N_BUCKETS - 1
    s = np.arange(ATT_TILE)[:, None]
    t = np.arange(ATT_TILE)[None, :]

    def table(dist):
        return jnp.transpose(rel_bias[jnp.asarray(_t5_bucket_np(dist))] - last, (2, 0, 1))

    causal = jnp.asarray(t >= s)
    pd = jnp.where(causal, table(t - s), NEG)
    p1 = table(ATT_TILE + t - s)
    w_old = jnp.where(jnp.asarray(s > t), 0.0, NEG).astype(F32)
    m = np.arange(CMP_NEAR_ROWS)[:, None] - CMP_NEAR_BACK
    dist_c = t - (CMP_STRIDE * m + CMP_BLOCK - 1)
    assert dist_c[0].min() > ATT_TILE
    pc = jnp.where(jnp.asarray(dist_c >= 0), table(dist_c), 0.0)
    return pd, p1, w_old, pc


def _cmp_attn_kernel(qt_ref, kc_ref, vct_ref, pc_ref, ovl_ref, ocmpt_ref, negm_ref, s_scr,
                     *, tq, n_sel, n_top):
    qi = pl.program_id(1)
    n_chunk = kc_ref.shape[2]
    n_i = lax.broadcasted_iota(jnp.int32, (n_chunk, tq), 0)
    t_i = qi * tq + lax.broadcasted_iota(jnp.int32, (n_chunk, tq), 1)
    valid = n_i * CMP_STRIDE + (CMP_BLOCK - 1) <= t_i
    jrow = lax.broadcasted_iota(jnp.int32, (n_sel, tq), 0)
    blk_t = (qi * tq + lax.broadcasted_iota(jnp.int32, (n_sel, tq), 1)) // SEL_BLOCK
    forced = (jrow == 0) | ((jrow <= blk_t) & (jrow > blk_t - N_LOCAL_BLOCKS))
    future = jrow > blk_t
    jl = lax.broadcasted_iota(jnp.int32, (SUBLANES, tq), 0)
    s_scr[0:CMP_NEAR_BACK, :] = jnp.zeros((CMP_NEAR_BACK, tq), F32)
    near = pl.ds(pl.multiple_of(qi * (tq // CMP_STRIDE), SUBLANES), CMP_NEAR_ROWS)
    body = slice(CMP_NEAR_BACK, CMP_NEAR_BACK + n_chunk)
    for hk in range(N_KV_HEADS):
        kc = kc_ref[0, hk]
        vct = vct_ref[0, hk]
        psum = jnp.zeros((n_chunk, tq), F32)
        for g in range(GQA):
            hd = hk * GQA + g
            rows = slice(hd * HEAD_DIM, (hd + 1) * HEAD_DIM)
            s_scr[body, :] = _dot(kc, qt_ref[0, rows, :])
            s_scr[near, :] = s_scr[near, :] + pc_ref[hd]
            s = jnp.where(valid, s_scr[body, :], NEG)
            m = jnp.max(s, axis=0, keepdims=True)
            e = jnp.where(valid, jnp.exp(s - m), 0.0)
            l = jnp.sum(e, axis=0, keepdims=True)
            p = e * jnp.where(l > 0.0, 1.0 / l, 0.0)
            psum = psum + p
            ocmpt_ref[0, rows, :] = _dot(vct, p.astype(BF16))
        imp = _dot(ovl_ref[...], psum, HIGHEST)
        v = jnp.where(forced, BIG, jnp.where(future, -BIG, imp))
        n_t = n_sel // SUBLANES
        vt = [v[a * SUBLANES:(a + 1) * SUBLANES] for a in range(n_t)]
        rank = [jnp.zeros((SUBLANES, tq), F32) for _ in range(n_t)]
        for jp in range(n_sel):
            row = jnp.broadcast_to(v[jp:jp + 1, :], (SUBLANES, tq))
            for a in range(n_t):
                if a > jp // SUBLANES:
                    hit = jnp.where(row >= vt[a], 1.0, 0.0)
                elif a < jp // SUBLANES:
                    hit = jnp.where(row > vt[a], 1.0, 0.0)
                else:
                    hit = jnp.where(jl > jp % SUBLANES, jnp.where(row >= vt[a], 1.0, 0.0),
                                    jnp.where(row > vt[a], 1.0, 0.0))
                rank[a] = rank[a] + hit
        rank = jnp.concatenate(rank, axis=0)
        negm_ref[0, hk] = jnp.where(rank < float(n_top), 0.0, NEG).astype(BF16)


def _cmp_attn(qt, kc, vct, pc, tq):
    bsz, d_attn, seq = qt.shape
    n_chunk = kc.shape[2]
    n_sel = seq // SEL_BLOCK
    n_top = min(SEL_TOPK, n_sel)
    n_cmp = (seq - CMP_BLOCK) // CMP_STRIDE + 1
    assert n_sel % SUBLANES == 0 and tq % CMP_STRIDE == 0 and (tq // CMP_STRIDE) % SUBLANES == 0
    cs = np.arange(n_chunk)[:, None] * CMP_STRIDE
    ss = np.arange(n_sel)[None, :] * SEL_BLOCK
    ovl = np.clip(np.minimum(cs + CMP_BLOCK, ss + SEL_BLOCK) - np.maximum(cs, ss), 0, None) / CMP_BLOCK
    ovl[n_cmp:] = 0.0
    ovl_t = jnp.asarray(ovl.T, dtype=F32)
    kern = functools.partial(_cmp_attn_kernel, tq=tq, n_sel=n_sel, n_top=n_top)
    return pl.pallas_call(
        kern,
        grid=(bsz, seq // tq),
        in_specs=[pl.BlockSpec((1, d_attn, tq), lambda b, i: (b, 0, i)),
                  pl.BlockSpec((1, N_KV_HEADS, n_chunk, HEAD_DIM), lambda b, i: (b, 0, 0, 0)),
                  pl.BlockSpec((1, N_KV_HEADS, HEAD_DIM, n_chunk), lambda b, i: (b, 0, 0, 0)),
                  pl.BlockSpec(pc.shape, lambda b, i: (0, 0, 0)),
                  pl.BlockSpec((n_sel, n_chunk), lambda b, i: (0, 0))],
        out_specs=[pl.BlockSpec((1, d_attn, tq), lambda b, i: (b, 0, i)),
                   pl.BlockSpec((1, N_KV_HEADS, n_sel, tq), lambda b, i: (b, 0, 0, i))],
        out_shape=[jax.ShapeDtypeStruct((bsz, d_attn, seq), F32),
                   jax.ShapeDtypeStruct((bsz, N_KV_HEADS, n_sel, seq), BF16)],
        scratch_shapes=[pltpu.VMEM((CMP_NEAR_BACK + n_chunk, tq), F32)],
        compiler_params=_cparams("parallel", "parallel"),
        name="cmp_attn",
    )(qt, kc, vct, pc, ovl_t)


def _sw_attn_kernel(qt_ref, negm_ref, ks_ref, vst_ref, kw_ref, vwt_ref, ocmpt_ref, gt_ref,
                    pd_ref, p1_ref, wold_ref, eye_ref, o_ref, qa_ref, acc_s, acc_w, *, tq, n_win_tiles):
    qi = pl.program_id(1)
    gates = _sigmoid(gt_ref[0])

    def span(kt, n=1):
        return pl.ds(pl.multiple_of(kt * (n * ATT_TILE), n * ATT_TILE), n * ATT_TILE)

    def col_max(s):
        return jnp.max(s, axis=0, keepdims=True)

    def col_sum(p):
        return jnp.sum(p, axis=0, keepdims=True)

    def softmax_first(s_list, v_list, acc, hk):
        m = functools.reduce(jnp.maximum, [col_max(s) for s in s_list])
        ps = [jnp.exp(s - m) for s in s_list]
        acc[hk] = sum(_dot(v, p.astype(BF16)) for v, p in zip(v_list, ps))
        return m, sum(col_sum(p) for p in ps)

    def softmax_update(m, l, s, v, acc, hk):
        m_new = jnp.maximum(m, col_max(s))
        alpha = jnp.exp(m - m_new)
        p = jnp.exp(s - m_new)
        acc[hk] = alpha * acc[hk] + _dot(v, p.astype(BF16))
        return m_new, alpha * l + col_sum(p)

    def exists(back):
        return jnp.where(qi >= back, 0.0, NEG)

    w_old = jnp.concatenate([wold_ref[...]] * GQA, axis=1)
    n_far = jnp.maximum(qi - 1, 0)
    state = []
    o_win = []
    for hk in range(N_KV_HEADS):
        heads = range(hk * GQA, (hk + 1) * GQA)
        q4 = jnp.concatenate([qt_ref[0, hd * HEAD_DIM:(hd + 1) * HEAD_DIM, :] for hd in heads], axis=1)
        nm = negm_ref[0, hk]
        if nm.shape[0] < LANES - HEAD_DIM:
            nm = jnp.concatenate([nm, jnp.zeros((LANES - HEAD_DIM - nm.shape[0], tq), BF16)], axis=0)
        qa = jnp.concatenate([q4, jnp.concatenate([nm] * GQA, axis=1)], axis=0)
        qa_ref[hk] = qa
        qw = jnp.concatenate([q4, jnp.zeros_like(q4)], axis=0)
        pd4 = jnp.concatenate([pd_ref[hd] for hd in heads], axis=1)
        p14 = jnp.concatenate([p1_ref[hd] for hd in heads], axis=1)

        kt1 = jnp.maximum(qi - 1, 0)
        s_near = [_dot(ks_ref[0, hk, span(qi), :], qa) + pd4,
                  _dot(ks_ref[0, hk, span(kt1), :], qa) + (p14 + exists(1))]
        v_near = [vst_ref[0, hk, :, span(qi)], vst_ref[0, hk, :, span(kt1)]]
        state += list(softmax_first(s_near, v_near, acc_s, hk))

        s_win = [_dot(kw_ref[0, hk, span(qi), :], qw) + pd4]
        v_win = [vwt_ref[0, hk, :, span(qi)]]
        for back in range(1, n_win_tiles + 1):
            kt = jnp.maximum(qi - back, 0)
            s = _dot(kw_ref[0, hk, span(kt), :], qw)
            if back == 1:
                s = s + (p14 + exists(back))
            elif back == n_win_tiles:
                s = s + (w_old + exists(back))
            else:
                s = s + exists(back)
            s_win.append(s)
            v_win.append(vwt_ref[0, hk, :, span(kt)])
        _, l_w = softmax_first(s_win, v_win, acc_w, hk)
        o_win.append(acc_w[hk] * (1.0 / l_w))

    def far(j, carry):
        out = []
        for hk in range(N_KV_HEADS):
            s = _dot(ks_ref[0, hk, span(j, 2), :], qa_ref[hk])
            out += list(softmax_update(carry[2 * hk], carry[2 * hk + 1], s, vst_ref[0, hk, :, span(j, 2)],
                                       acc_s, hk))
        return tuple(out)

    state = lax.fori_loop(0, n_far // 2, far, tuple(state))
    k_odd = jnp.maximum(n_far - 1, 0)
    odd = jnp.where(n_far % 2 == 1, 0.0, NEG)

    ys = []
    for hk in range(N_KV_HEADS):
        s = _dot(ks_ref[0, hk, span(k_odd), :], qa_ref[hk]) + odd
        _, l_s = softmax_update(state[2 * hk], state[2 * hk + 1], s, vst_ref[0, hk, :, span(k_odd)], acc_s, hk)
        o_sel = acc_s[hk] * (1.0 / l_s)
        for g in range(GQA):
            hd = hk * GQA + g
            cols = slice(g * tq, (g + 1) * tq)
            ys.append(gates[3 * hd:3 * hd + 1] * ocmpt_ref[0, hd * HEAD_DIM:(hd + 1) * HEAD_DIM, :]
                      + gates[3 * hd + 1:3 * hd + 2] * o_sel[:, cols]
                      + gates[3 * hd + 2:3 * hd + 3] * o_win[hk][:, cols])
    yt = jnp.concatenate(ys, axis=0).astype(BF16)
    o_ref[0] = _dot_nt(eye_ref[...], yt).astype(BF16)


def _sw_attn(qt, negm, ks, vst, kw, vwt, ocmpt, gt, pd, p1, w_old, tq):
    bsz, d_attn, seq = qt.shape
    n_sel = negm.shape[2]
    assert tq == ATT_TILE and WINDOW % ATT_TILE == 0 and n_sel <= LANES - HEAD_DIM
    eye = jnp.eye(tq, dtype=BF16)
    kern = functools.partial(_sw_attn_kernel, tq=tq, n_win_tiles=WINDOW // ATT_TILE)
    k_spec = pl.BlockSpec((1, N_KV_HEADS, seq, LANES), lambda b, i: (b, 0, 0, 0))
    vt_spec = pl.BlockSpec((1, N_KV_HEADS, HEAD_DIM, seq), lambda b, i: (b, 0, 0, 0))
    qt_spec = pl.BlockSpec((1, d_attn, tq), lambda b, i: (b, 0, i))
    full = lambda shape: pl.BlockSpec(shape, lambda b, i: (0,) * len(shape))
    return pl.pallas_call(
        kern,
        grid=(bsz, seq // tq),
        in_specs=[qt_spec,
                  pl.BlockSpec((1, N_KV_HEADS, n_sel, tq), lambda b, i: (b, 0, 0, i)),
                  k_spec, vt_spec, k_spec, vt_spec,
                  qt_spec,
                  pl.BlockSpec((1, gt.shape[1], tq), lambda b, i: (b, 0, i)),
                  full(pd.shape), full(p1.shape), full(w_old.shape), full(eye.shape)],
        out_specs=pl.BlockSpec((1, tq, d_attn), lambda b, i: (b, i, 0)),
        out_shape=jax.ShapeDtypeStruct((bsz, seq, d_attn), BF16),
        scratch_shapes=[pltpu.VMEM((N_KV_HEADS, LANES, GQA * tq), BF16),
                        pltpu.VMEM((N_KV_HEADS, HEAD_DIM, GQA * tq), F32),
                        pltpu.VMEM((N_KV_HEADS, HEAD_DIM, GQA * tq), F32)],
        compiler_params=_cparams("parallel", "parallel"),
        name="sel_win_attn",
    )(qt, negm, ks, vst, kw, vwt, ocmpt, gt, pd, p1, w_old, eye)


def _outproj_kernel(x_ref, ys_ref, ya_ref, ws_ref, wa_ref, o_ref):
    o_ref[0] = x_ref[0] + _dot(ys_ref[...], ws_ref[...]) + _dot(ya_ref[0], wa_ref[...])


def _outproj(x, y_ssm_tm, y_attn, w_out, tm):
    bsz, seq, d = x.shape
    d_ssm = y_ssm_tm.shape[1] // bsz
    d_attn = y_attn.shape[2]
    wb = w_out.astype(BF16)
    ws, wa = wb[:d_ssm], wb[d_ssm:]
    full = lambda shape: pl.BlockSpec(shape, lambda b, i: (0,) * len(shape))
    return pl.pallas_call(
        _outproj_kernel,
        grid=(bsz, seq // tm),
        in_specs=[pl.BlockSpec((1, tm, d), lambda b, i: (b, i, 0)),
                  pl.BlockSpec((tm, d_ssm), lambda b, i: (i, b)),
                  pl.BlockSpec((1, tm, d_attn), lambda b, i: (b, i, 0)),
                  full(ws.shape), full(wa.shape)],
        out_specs=pl.BlockSpec((1, tm, d), lambda b, i: (b, i, 0)),
        out_shape=jax.ShapeDtypeStruct((bsz, seq, d), F32),
        compiler_params=_cparams("parallel", "parallel"),
        name="outproj",
    )(x, y_ssm_tm, y_attn, ws, wa)


def _router_kernel(x_ref, g_ref, wr_ref, br_ref, h_ref, route_ref):
    x = x_ref[...]
    ms = jnp.mean(x * x, axis=-1, keepdims=True)
    h = x * lax.rsqrt(ms + EPS) * g_ref[...]
    h_ref[...] = h.astype(BF16)
    logits = _dot(h, wr_ref[...], HIGHEST) + br_ref[...]
    lane = lax.broadcasted_iota(jnp.int32, logits.shape, 1)
    is_grp = (lane >= N_EXPERTS) & (lane < N_EXPERTS + N_GROUPS)
    lg = jnp.where(is_grp, logits, NEG)
    mg = jnp.max(lg, axis=-1, keepdims=True)
    g_val = 1.0 / jnp.sum(jnp.where(is_grp, jnp.exp(lg - mg), 0.0), axis=-1, keepdims=True)
    g_idx = jnp.min(jnp.where(is_grp & (lg == mg), lane - N_EXPERTS, N_GROUPS), axis=-1, keepdims=True)
    in_grp = (lane >= g_idx * EXPERTS_PER_GROUP) & (lane < (g_idx + 1) * EXPERTS_PER_GROUP)
    le = jnp.where(in_grp, logits, NEG)
    me = jnp.max(le, axis=-1, keepdims=True)
    ee = jnp.where(in_grp, jnp.exp(le - me), 0.0)
    pe = ee / jnp.sum(ee, axis=-1, keepdims=True)
    p1 = jnp.max(pe, axis=-1, keepdims=True)
    i1 = jnp.min(jnp.where(in_grp & (pe == p1), lane, LANES), axis=-1, keepdims=True)
    rest = in_grp & (lane != i1)
    pr = jnp.where(rest, pe, -1.0)
    p2 = jnp.max(pr, axis=-1, keepdims=True)
    i2 = jnp.min(jnp.where(rest & (pr == p2), lane, LANES), axis=-1, keepdims=True)
    tot = p1 + p2
    route_ref[...] = jnp.where(lane == 0, i1.astype(F32),
                               jnp.where(lane == 1, i2.astype(F32),
                                         jnp.where(lane == 2, g_val * (p1 / tot),
                                                   jnp.where(lane == 3, g_val * (p2 / tot), 0.0))))


def _router(x2d, g, rg_w, rg_b, re_w, re_b, tm):
    tok, d = x2d.shape
    we = jnp.transpose(re_w, (1, 0, 2)).reshape(d, N_EXPERTS)
    wr = jnp.pad(jnp.concatenate([we, rg_w], axis=1), ((0, 0), (0, LANES - N_EXPERTS - N_GROUPS)))
    br = jnp.pad(jnp.concatenate([re_b.reshape(-1), rg_b]), (0, LANES - N_EXPERTS - N_GROUPS)).reshape(1, LANES)
    full = lambda shape: pl.BlockSpec(shape, lambda i: (0,) * len(shape))
    return pl.pallas_call(
        _router_kernel,
        grid=(tok // tm,),
        in_specs=[pl.BlockSpec((tm, d), lambda i: (i, 0)), full((1, d)), full(wr.shape), full(br.shape)],
        out_specs=[pl.BlockSpec((tm, d), lambda i: (i, 0)), pl.BlockSpec((tm, LANES), lambda i: (i, 0))],
        out_shape=[jax.ShapeDtypeStruct((tok, d), BF16), jax.ShapeDtypeStruct((tok, LANES), F32)],
        compiler_params=_cparams("parallel"),
        name="router",
    )(x2d, g.reshape(1, d), wr, br)


MOE_ROW_TILE = 256


def _dispatch_plan(route, tok):
    n_pairs = 2 * tok
    pe = jnp.concatenate([route[:, 0], route[:, 1]]).astype(jnp.int32)
    pw = jnp.concatenate([route[:, 2], route[:, 3]])
    ptok = jnp.concatenate([jnp.arange(tok, dtype=jnp.int32)] * 2)
    onehot = (pe[:, None] == jnp.arange(N_EXPERTS, dtype=jnp.int32)[None, :]).astype(jnp.int32)
    csum = jnp.cumsum(onehot, axis=0)
    counts = csum[-1]
    cend = jnp.cumsum(counts)
    cstart = cend - counts
    dest = jnp.sum(onehot * (cstart[None, :] + csum - 1), axis=1)
    _, tok_sorted, w_sorted = lax.sort((pe, ptok, pw), num_keys=1, is_stable=True)
    first_tile = cstart // MOE_ROW_TILE
    last_tile = (cend - 1) // MOE_ROW_TILE
    n_vis = jnp.where(counts > 0, last_tile - first_tile + 1, 0)
    vis_end = jnp.cumsum(n_vis)
    n_visits = vis_end[-1]
    max_visits = n_pairs // MOE_ROW_TILE + N_EXPERTS - 1
    v = jnp.minimum(jnp.arange(max_visits, dtype=jnp.int32), n_visits - 1)
    ve = jnp.searchsorted(vis_end, v, side='right').astype(jnp.int32)
    vt = first_tile[ve] + (v - (vis_end[ve] - n_vis[ve]))
    vfirst = jnp.concatenate([jnp.ones((1,), jnp.int32), (vt[1:] != vt[:-1]).astype(jnp.int32)])
    meta = jnp.stack([ve, vt.astype(jnp.int32), vfirst, cstart[ve], cend[ve]]).astype(jnp.int32)
    return dest, tok_sorted, w_sorted, meta, n_visits.reshape(1).astype(jnp.int32)


def _moe_kernel(meta_ref, nv_ref, x_ref, w_ref, wgu_ref, wd_ref, o_ref):
    v = pl.program_id(0)

    @pl.when(v < nv_ref[0])
    def _():
        f = wd_ref.shape[1]
        row = meta_ref[1, v] * MOE_ROW_TILE + lax.broadcasted_iota(jnp.int32, (MOE_ROW_TILE, 1), 0)
        mine = (row >= meta_ref[3, v]) & (row < meta_ref[4, v])
        gu = _dot(x_ref[...], wgu_ref[0])
        a = gu[:, :f]
        hid = jnp.where(mine, (a * _sigmoid(a)) * gu[:, f:] * w_ref[...], 0.0)
        y = _dot(hid.astype(BF16), wd_ref[0])

        @pl.when(meta_ref[2, v] == 1)
        def _():
            o_ref[...] = y

        @pl.when(meta_ref[2, v] == 0)
        def _():
            o_ref[...] += y


def _moe_experts(x_sorted, w_sorted, meta, n_visits, w_gate, w_up, w_down):
    rows, d = x_sorted.shape
    f = w_gate.shape[-1]
    wgu = jnp.concatenate([w_gate.reshape(N_EXPERTS, d, f), w_up.reshape(N_EXPERTS, d, f)], axis=-1).astype(BF16)
    wd = w_down.reshape(N_EXPERTS, f, d).astype(BF16)
    grid_spec = pltpu.PrefetchScalarGridSpec(
        num_scalar_prefetch=2,
        grid=(meta.shape[1],),
        in_specs=[pl.BlockSpec((MOE_ROW_TILE, d), lambda v, m, n: (m[1, v], 0)),
                  pl.BlockSpec((MOE_ROW_TILE, 1), lambda v, m, n: (m[1, v], 0)),
                  pl.BlockSpec((1, d, 2 * f), lambda v, m, n: (m[0, v], 0, 0)),
                  pl.BlockSpec((1, f, d), lambda v, m, n: (m[0, v], 0, 0))],
        out_specs=pl.BlockSpec((MOE_ROW_TILE, d), lambda v, m, n: (m[1, v], 0)),
    )
    return pl.pallas_call(
        _moe_kernel,
        grid_spec=grid_spec,
        out_shape=jax.ShapeDtypeStruct((rows, d), F32),
        compiler_params=_cparams("arbitrary"),
        name="moe_experts",
    )(meta, n_visits, x_sorted, w_sorted.reshape(rows, 1), wgu, wd)


def _combine_kernel(x_ref, y1_ref, y2_ref, g_ref, o_ref, *, final_norm):
    x = x_ref[...] + y1_ref[...] + y2_ref[...]
    if final_norm:
        ms = jnp.mean(x * x, axis=-1, keepdims=True)
        x = x * lax.rsqrt(ms + EPS) * g_ref[...]
    o_ref[...] = x


def _combine(x2d, y1, y2, g, final_norm, tm):
    tok, d = x2d.shape
    blk = pl.BlockSpec((tm, d), lambda i: (i, 0))
    return pl.pallas_call(
        functools.partial(_combine_kernel, final_norm=final_norm),
        grid=(tok // tm,),
        in_specs=[blk, blk, blk, pl.BlockSpec((1, d), lambda i: (0, 0))],
        out_specs=blk,
        out_shape=jax.ShapeDtypeStruct((tok, d), F32),
        compiler_params=_cparams("parallel"),
        name="moe_combine",
    )(x2d, y1, y2, g.reshape(1, d))


def _moe(h, route, x2d, w_gate, w_up, w_down, g_final, final_norm, tm):
    tok = x2d.shape[0]
    dest, tok_sorted, w_sorted, meta, n_visits = _dispatch_plan(route, tok)
    x_sorted = jnp.take(h, tok_sorted, axis=0)
    y_sorted = _moe_experts(x_sorted, w_sorted, meta, n_visits, w_gate, w_up, w_down)
    y1 = jnp.take(y_sorted, dest[:tok], axis=0)
    y2 = jnp.take(y_sorted, dest[tok:], axis=0)
    return _combine(x2d, y1, y2, g_final, final_norm, tm)


def kernel(x, norm1_g, norm2_g, final_g, w_in, w_out, ssm_a_re, ssm_a_im, ssm_b_re, ssm_b_im, ssm_c_re, ssm_c_im, ssm_d, ssm_log_dt, ssm_w_glu, cmp_pos, cmp_w1, cmp_w2, rel_bias, router_g_w, router_g_b, router_e_w, router_e_b, exp_w_gate, exp_w_up, exp_w_down):
    bsz, seq, d = x.shape
    depth = norm1_g.shape[0]
    tm = min(512, seq)
    tok_tile = min(1024, bsz * seq)
    pd, p1, w_old, pc = _bias_tables(rel_bias, seq)
    for l in range(depth):
        u_tm, k_cmp, v_cmp, ks, kw, qt, vst, vwt, gt = _inproj(x, norm1_g[l], w_in[l], tm)
        y_ssm = _s5(u_tm, bsz, ssm_a_re[l], ssm_a_im[l], ssm_b_re[l], ssm_b_im[l], ssm_c_re[l],
                    ssm_c_im[l], ssm_d[l].reshape(-1), ssm_log_dt[l], ssm_w_glu[l], tc=min(32, seq))
        kc, vct = _compress(k_cmp, v_cmp, cmp_pos[l], cmp_w1[l], cmp_w2[l])
        ocmpt, negm = _cmp_attn(qt, kc, vct, pc, ATT_TILE)
        y_attn = _sw_attn(qt, negm, ks, vst, kw, vwt, ocmpt, gt, pd, p1, w_old, ATT_TILE)
        x = _outproj(x, y_ssm, y_attn, w_out[l], tm)
        x2d = x.reshape(bsz * seq, d)
        h, route = _router(x2d, norm2_g[l], router_g_w[l], router_g_b[l], router_e_w[l], router_e_b[l], tok_tile)
        x = _moe(h, route, x2d, exp_w_gate[l], exp_w_up[l], exp_w_down[l], final_g, l == depth - 1,
                 tok_tile).reshape(bsz, seq, d)
    return x
```

```python
import functools
import math

import numpy as np
import jax
import jax.numpy as jnp
from jax import lax
from jax.experimental import pallas as pl
from jax.experimental.pallas import tpu as pltpu

F32 = jnp.float32
BF16 = jnp.bfloat16

SSM_GROUP = 16
SSM_STATE = 64
HEAD_DIM = 64
N_KV_HEADS = 2
GQA = 4
N_HEADS = N_KV_HEADS * GQA
CMP_BLOCK = 32
CMP_STRIDE = 16
SEL_BLOCK = 64
SEL_TOPK = 16
N_LOCAL_BLOCKS = 2
WINDOW = 512
N_BUCKETS = 32
MAX_DISTANCE = 128
N_GROUPS = 4
EXPERTS_PER_GROUP = 8
N_EXPERTS = N_GROUPS * EXPERTS_PER_GROUP
EPS = 1e-6
NEG = -1e30
BIG = 1e9

LANES = 128
SUBLANES = 8
ATT_TILE = 128
V_ROWS = HEAD_DIM + 16
STALE_MAX_SLACK = 20.0
SSM_OCT = 8
VMEM_LIMIT = 56 * 1024 * 1024

HIGHEST = lax.Precision.HIGHEST


def _cparams(*sem):
    return pltpu.CompilerParams(dimension_semantics=sem, vmem_limit_bytes=VMEM_LIMIT)


def _dot(a, b, precision=None):
    return jnp.dot(a, b, preferred_element_type=F32, precision=precision)


def _dot_nt(a, b, precision=None):
    return lax.dot_general(a, b, (((1,), (1,)), ((), ())),
                           preferred_element_type=F32, precision=precision)


def _gelu_tanh(x):
    return 0.5 * x * (1.0 + jnp.tanh(math.sqrt(2.0 / math.pi) * (x + 0.044715 * (x * x * x))))


def _sigmoid(x):
    return 1.0 / (1.0 + jnp.exp(-x))


def _t5_bucket_np(dist):
    n = np.maximum(dist, 0)
    max_exact = N_BUCKETS // 2
    nf = np.maximum(n, 1).astype(np.float32)
    large = max_exact + (np.log(nf / np.float32(max_exact))
                         / np.float32(math.log(MAX_DISTANCE / max_exact))
                         * np.float32(N_BUCKETS - max_exact)).astype(np.int32)
    large = np.minimum(large, N_BUCKETS - 1)
    return np.where(n < max_exact, n, large).astype(np.int32)


def _inproj_kernel(x_ref, g_ref, wu_ref, wc_ref, wk_ref, wqt_ref, wvt_ref, wgt_ref,
                   u_ref, kc_ref, vc_ref, ks_ref, kw_ref, qt_ref, vst_ref, vwt_ref, gt_ref, *, tm):
    x = x_ref[0]
    ms = jnp.mean(x * x, axis=-1, keepdims=True)
    h = (x * lax.rsqrt(ms + EPS) * g_ref[...]).astype(BF16)
    u_ref[...] = _dot(h, wu_ref[...])
    dkv = N_KV_HEADS * HEAD_DIM
    c = _dot(h, wc_ref[...])
    kc_ref[0] = c[:, :dkv]
    vc_ref[0] = c[:, dkv:]
    ka = _dot(h, wk_ref[...])
    blk = (pl.program_id(1) * tm + lax.broadcasted_iota(jnp.int32, (tm, LANES), 0)) // SEL_BLOCK
    onehot = jnp.where(lax.broadcasted_iota(jnp.int32, (tm, LANES), 1) - HEAD_DIM == blk, 1.0, 0.0)
    for hk in range(N_KV_HEADS):
        ks_ref[0, hk] = (ka[:, hk * LANES:(hk + 1) * LANES] + onehot).astype(BF16)
        kw_ref[0, hk] = ka[:, (N_KV_HEADS + hk) * LANES:(N_KV_HEADS + hk + 1) * LANES].astype(BF16)
    qt_ref[0] = (_dot_nt(wqt_ref[...], h) * (HEAD_DIM ** -0.5)).astype(BF16)
    vt = _dot_nt(wvt_ref[...], h).astype(BF16)
    ones_rows = jnp.where(lax.broadcasted_iota(jnp.int32, (V_ROWS - HEAD_DIM, tm), 0) == 0, 1.0, 0.0).astype(BF16)
    for hk in range(N_KV_HEADS):
        vst_ref[0, hk] = jnp.concatenate([vt[hk * HEAD_DIM:(hk + 1) * HEAD_DIM], ones_rows], axis=0)
        vwt_ref[0, hk] = jnp.concatenate([vt[dkv + hk * HEAD_DIM:dkv + (hk + 1) * HEAD_DIM], ones_rows], axis=0)
    gt_ref[0] = _dot_nt(wgt_ref[...], h)


def _inproj(x, g, w_in, tm):
    bsz, seq, d = x.shape
    d_ssm = d // 2
    d_attn = d - d_ssm
    dkv = N_KV_HEADS * HEAD_DIM
    o_kv = d_ssm + d_attn
    o_g = o_kv + 6 * dkv
    assert seq // SEL_BLOCK <= LANES - HEAD_DIM
    wb = w_in.astype(BF16)
    wu = wb[:, :d_ssm]
    wq = wb[:, d_ssm:o_kv]
    k_cmp, v_cmp, k_sel, v_sel, k_win, v_win = [wb[:, o_kv + i * dkv:o_kv + (i + 1) * dkv] for i in range(6)]
    wc = jnp.concatenate([k_cmp, v_cmp], axis=1)

    def pad_heads(w):
        w = w.reshape(d, N_KV_HEADS, HEAD_DIM)
        return jnp.pad(w, ((0, 0), (0, 0), (0, LANES - HEAD_DIM))).reshape(d, N_KV_HEADS * LANES)

    wk = jnp.concatenate([pad_heads(k_sel), pad_heads(k_win)], axis=1)
    wqt = wq.T
    wvt = jnp.concatenate([v_sel, v_win], axis=1).T
    n_gate_rows = 4 * SUBLANES
    wgt = jnp.pad(wb[:, o_g:], ((0, 0), (0, n_gate_rows - 3 * N_HEADS))).T
    full = lambda shape: pl.BlockSpec(shape, lambda b, i: (0,) * len(shape))
    k_spec = pl.BlockSpec((1, N_KV_HEADS, tm, LANES), lambda b, i: (b, 0, i, 0))
    vt_spec = pl.BlockSpec((1, N_KV_HEADS, V_ROWS, tm), lambda b, i: (b, 0, 0, i))
    k_shape = jax.ShapeDtypeStruct((bsz, N_KV_HEADS, seq, LANES), BF16)
    vt_shape = jax.ShapeDtypeStruct((bsz, N_KV_HEADS, V_ROWS, seq), BF16)
    return pl.pallas_call(
        functools.partial(_inproj_kernel, tm=tm),
        grid=(bsz, seq // tm),
        in_specs=[pl.BlockSpec((1, tm, d), lambda b, i: (b, i, 0)),
                  full((1, d)), full(wu.shape), full(wc.shape), full(wk.shape),
                  full(wqt.shape), full(wvt.shape), full(wgt.shape)],
        out_specs=[pl.BlockSpec((tm, d_ssm), lambda b, i: (i, b)),
                   pl.BlockSpec((1, tm, dkv), lambda b, i: (b, i, 0)),
                   pl.BlockSpec((1, tm, dkv), lambda b, i: (b, i, 0)),
                   k_spec, k_spec,
                   pl.BlockSpec((1, d_attn, tm), lambda b, i: (b, 0, i)),
                   vt_spec, vt_spec,
                   pl.BlockSpec((1, n_gate_rows, tm), lambda b, i: (b, 0, i))],
        out_shape=[jax.ShapeDtypeStruct((seq, bsz * d_ssm), F32),
                   jax.ShapeDtypeStruct((bsz, seq, dkv), F32),
                   jax.ShapeDtypeStruct((bsz, seq, dkv), F32),
                   k_shape, k_shape,
                   jax.ShapeDtypeStruct((bsz, d_attn, seq), BF16),
                   vt_shape, vt_shape,
                   jax.ShapeDtypeStruct((bsz, n_gate_rows, seq), F32)],
        compiler_params=_cparams("parallel", "parallel"),
        name="inproj",
    )(x, g.reshape(1, d), wu, wc, wk, wqt, wvt, wgt)


def _s5_kernel(u_ref, bmat_ref, cmat_ref, lam_ref, d_ref, wglu_ref, y_ref, bu_ref, st_ref,
               *, tc, bsz, n_oct, oct_c, oct_s, lane_chunk):
    rows = tc * bsz
    d_ssm = n_oct * oct_c
    n_state = n_oct * oct_s

    @pl.when(pl.program_id(0) == 0)
    def _():
        st_ref[...] = jnp.zeros_like(st_ref)

    u = u_ref[...].reshape(rows, d_ssm)
    ub = u.astype(BF16)
    for o in range(n_oct):
        r = _dot(ub[:, o * oct_c:(o + 1) * oct_c], bmat_ref[o])
        bu_ref[:, o * oct_s:(o + 1) * oct_s] = r[:, :oct_s]
        bu_ref[:, n_state + o * oct_s:n_state + (o + 1) * oct_s] = r[:, oct_s:]

    for c0 in range(0, n_state, lane_chunk):
        re = slice(c0, c0 + lane_chunk)
        im = slice(n_state + c0, n_state + c0 + lane_chunk)

        def step(t, carry, re=re, im=im):
            sr, si = carry
            row = pl.multiple_of(t * bsz, bsz)
            lr = lam_ref[:, re]
            li = lam_ref[:, im]
            nr = lr * sr - li * si + bu_ref[pl.ds(row, bsz), re]
            ni = lr * si + li * sr + bu_ref[pl.ds(row, bsz), im]
            bu_ref[pl.ds(row, bsz), re] = nr
            bu_ref[pl.ds(row, bsz), im] = ni
            return nr, ni

        sr, si = lax.fori_loop(0, tc, step, (st_ref[:, re], st_ref[:, im]))
        st_ref[:, re] = sr
        st_ref[:, im] = si

    ys = []
    for o in range(n_oct):
        s_re = bu_ref[:, o * oct_s:(o + 1) * oct_s].astype(BF16)
        s_im = bu_ref[:, n_state + o * oct_s:n_state + (o + 1) * oct_s].astype(BF16)
        ys.append(_dot(s_re, cmat_ref[o, :oct_s]) + _dot(s_im, cmat_ref[o, oct_s:]))
    y = jnp.concatenate(ys, axis=1) + d_ref[...] * u
    y = _gelu_tanh(y).astype(BF16)
    z = _dot(y, wglu_ref[...])
    out = z[:, :d_ssm] * _sigmoid(z[:, d_ssm:])
    y_ref[...] = out.astype(BF16).reshape(tc, bsz, d_ssm)


def _s5_params(a_re, a_im, b_re, b_im, c_re, c_im, log_dt, bsz):
    g, p = a_re.shape
    h = b_re.shape[-1]
    n_oct = g // SSM_OCT
    dt = jnp.exp(log_dt)[:, None]
    mag = jnp.exp(a_re * dt)
    lr = mag * jnp.cos(a_im * dt)
    li = mag * jnp.sin(a_im * dt)
    den = a_re * a_re + a_im * a_im
    kr = ((lr - 1.0) * a_re + li * a_im) / den
    ki = (li * a_re - (lr - 1.0) * a_im) / den
    bb_re = kr[..., None] * b_re - ki[..., None] * b_im
    bb_im = kr[..., None] * b_im + ki[..., None] * b_re
    eye = jnp.eye(SSM_OCT, dtype=F32)

    def bd_in(m):
        m = m.reshape(n_oct, SSM_OCT, p, h)
        return jnp.einsum('ogph,gk->oghkp', m, eye).reshape(n_oct, SSM_OCT * h, SSM_OCT * p)

    def bd_out(m):
        m = m.reshape(n_oct, SSM_OCT, h, p)
        return jnp.einsum('oghp,gk->ogpkh', m, eye).reshape(n_oct, SSM_OCT * p, SSM_OCT * h)

    bmat = jnp.concatenate([bd_in(bb_re), bd_in(bb_im)], axis=-1).astype(BF16)
    cmat = jnp.concatenate([bd_out(c_re), bd_out(-c_im)], axis=1).astype(BF16)
    lam = jnp.concatenate([lr.reshape(1, g * p), li.reshape(1, g * p)], axis=1)
    lam = jnp.broadcast_to(lam, (bsz, 2 * g * p))
    return bmat, cmat, lam


def _s5(u_tm, bsz, a_re, a_im, b_re, b_im, c_re, c_im, d, log_dt, w_glu, tc):
    seq = u_tm.shape[0]
    d_ssm = u_tm.shape[1] // bsz
    g, p = a_re.shape
    n_oct = g // SSM_OCT
    oct_c = SSM_OCT * SSM_GROUP
    oct_s = SSM_OCT * p
    bmat, cmat, lam = _s5_params(a_re, a_im, b_re, b_im, c_re, c_im, log_dt, bsz)
    u3 = u_tm.reshape(seq, bsz, d_ssm)
    full = lambda shape: pl.BlockSpec(shape, lambda i: (0,) * len(shape))
    kern = functools.partial(_s5_kernel, tc=tc, bsz=bsz, n_oct=n_oct, oct_c=oct_c, oct_s=oct_s,
                             lane_chunk=min(1024, g * p))
    y = pl.pallas_call(
        kern,
        grid=(seq // tc,),
        in_specs=[pl.BlockSpec((tc, bsz, d_ssm), lambda i: (i, 0, 0)),
                  full(bmat.shape), full(cmat.shape), full(lam.shape),
                  full((1, d_ssm)), full(w_glu.shape)],
        out_specs=pl.BlockSpec((tc, bsz, d_ssm), lambda i: (i, 0, 0)),
        out_shape=jax.ShapeDtypeStruct((seq, bsz, d_ssm), BF16),
        scratch_shapes=[pltpu.VMEM((tc * bsz, 2 * g * p), F32),
                        pltpu.VMEM((bsz, 2 * g * p), F32)],
        compiler_params=_cparams("arbitrary"),
        name="s5_mixer",
    )(u3, bmat, cmat, lam, d.reshape(1, d_ssm), w_glu.astype(BF16))
    return y.reshape(seq, bsz * d_ssm)


def _compress_kernel(k_ref, v_ref, w1a_ref, w1b_ref, posw_ref, w2_ref, w2t_ref, kc_ref, vct_ref):
    def hidden(i, src):
        x = src[0]
        a = _dot(x, w1a_ref[i], HIGHEST)
        b = _dot(x, w1b_ref[i], HIGHEST)
        return _gelu_tanh(a + pltpu.roll(b, a.shape[0] - 1, 0) + posw_ref[i])

    kc = _dot(hidden(0, k_ref), w2_ref[...], HIGHEST).astype(BF16)
    vct = _dot_nt(w2t_ref[...], hidden(1, v_ref), HIGHEST).astype(BF16)
    for hk in range(N_KV_HEADS):
        kc_ref[0, hk] = kc[:, hk * HEAD_DIM:(hk + 1) * HEAD_DIM]
        vct_ref[0, hk] = vct[hk * HEAD_DIM:(hk + 1) * HEAD_DIM]


def _compress(k_cmp, v_cmp, cmp_pos, cmp_w1, cmp_w2):
    bsz, seq, dkv = k_cmp.shape
    n_chunk = seq // CMP_STRIDE
    kview = k_cmp.reshape(bsz, n_chunk, CMP_STRIDE * dkv)
    vview = v_cmp.reshape(bsz, n_chunk, CMP_STRIDE * dkv)
    eye = jnp.eye(N_KV_HEADS, dtype=F32)
    w1 = jnp.einsum('ilde,hk->ilhdke', cmp_w1, eye)
    w1 = w1.reshape(2, CMP_BLOCK, dkv, dkv)
    w1a = w1[:, :CMP_STRIDE].reshape(2, CMP_STRIDE * dkv, dkv)
    w1b = w1[:, CMP_STRIDE:].reshape(2, CMP_STRIDE * dkv, dkv)
    posw = jnp.einsum('ild,ilde->ie', cmp_pos, cmp_w1, precision=HIGHEST)
    posw = jnp.tile(posw, (1, N_KV_HEADS)).reshape(2, 1, dkv)
    w2 = jnp.einsum('ief,hk->ihekf', cmp_w2, eye).reshape(2, dkv, dkv)
    full = lambda shape: pl.BlockSpec(shape, lambda b: (0,) * len(shape))
    return pl.pallas_call(
        _compress_kernel,
        grid=(bsz,),
        in_specs=[pl.BlockSpec((1, n_chunk, CMP_STRIDE * dkv), lambda b: (b, 0, 0)),
                  pl.BlockSpec((1, n_chunk, CMP_STRIDE * dkv), lambda b: (b, 0, 0)),
                  full(w1a.shape), full(w1b.shape), full(posw.shape), full((dkv, dkv)), full((dkv, dkv))],
        out_specs=[pl.BlockSpec((1, N_KV_HEADS, n_chunk, HEAD_DIM), lambda b: (b, 0, 0, 0)),
                   pl.BlockSpec((1, N_KV_HEADS, HEAD_DIM, n_chunk), lambda b: (b, 0, 0, 0))],
        out_shape=[jax.ShapeDtypeStruct((bsz, N_KV_HEADS, n_chunk, HEAD_DIM), BF16),
                   jax.ShapeDtypeStruct((bsz, N_KV_HEADS, HEAD_DIM, n_chunk), BF16)],
        compiler_params=_cparams("parallel"),
        name="compress",
    )(kview, vview, w1a, w1b, posw, w2[0], w2[1].T)


CMP_NEAR_BACK = 2 * SUBLANES
CMP_NEAR_ROWS = 3 * SUBLANES


def _bias_tables(rel_bias, seq):
    last = rel_bias[N_BUCKETS - 1]
    assert int(_t5_bucket_np(np.arange(ATT_TILE + 1, max(seq, 2 * ATT_TILE))).min()) == N_BUCKETS - 1
    s = np.arange(ATT_TILE)[:, None]
    t = np.arange(ATT_TILE)[None, :]

    def table(dist):
        return jnp.transpose(rel_bias[jnp.asarray(_t5_bucket_np(dist))] - last, (2, 0, 1))

    causal = jnp.asarray(t >= s)
    pd = jnp.where(causal, table(t - s), NEG)
    p1 = table(ATT_TILE + t - s)
    w_old = jnp.where(jnp.asarray(s > t), 0.0, NEG).astype(F32)
    m = np.arange(CMP_NEAR_ROWS)[:, None] - CMP_NEAR_BACK
    dist_c = t - (CMP_STRIDE * m + CMP_BLOCK - 1)
    assert dist_c[0].min() > ATT_TILE
    pc = jnp.where(jnp.asarray(dist_c >= 0), table(dist_c), 0.0)
    return pd, p1, w_old, pc


def _cmp_attn_kernel(qt_ref, kc_ref, vct_ref, pc_ref, ovl_ref, ocmpt_ref, negm_ref, s_scr,
                     *, tq, n_sel, n_top):
    qi = pl.program_id(1)
    n_chunk = kc_ref.shape[2]
    n_i = lax.broadcasted_iota(jnp.int32, (n_chunk, tq), 0)
    t_i = qi * tq + lax.broadcasted_iota(jnp.int32, (n_chunk, tq), 1)
    valid = n_i * CMP_STRIDE + (CMP_BLOCK - 1) <= t_i
    jrow = lax.broadcasted_iota(jnp.int32, (n_sel, tq), 0)
    blk_t = (qi * tq + lax.broadcasted_iota(jnp.int32, (n_sel, tq), 1)) // SEL_BLOCK
    forced = (jrow == 0) | ((jrow <= blk_t) & (jrow > blk_t - N_LOCAL_BLOCKS))
    future = jrow > blk_t
    jl = lax.broadcasted_iota(jnp.int32, (SUBLANES, tq), 0)
    s_scr[0:CMP_NEAR_BACK, :] = jnp.zeros((CMP_NEAR_BACK, tq), F32)
    near = pl.ds(pl.multiple_of(qi * (tq // CMP_STRIDE), SUBLANES), CMP_NEAR_ROWS)
    body = slice(CMP_NEAR_BACK, CMP_NEAR_BACK + n_chunk)
    for hk in range(N_KV_HEADS):
        kc = kc_ref[0, hk]
        vct = vct_ref[0, hk]
        psum = jnp.zeros((n_chunk, tq), F32)
        for g in range(GQA):
            hd = hk * GQA + g
            rows = slice(hd * HEAD_DIM, (hd + 1) * HEAD_DIM)
            s_scr[body, :] = _dot(kc, qt_ref[0, rows, :])
            s_scr[near, :] = s_scr[near, :] + pc_ref[hd]
            s = jnp.where(valid, s_scr[body, :], NEG)
            m = jnp.max(s, axis=0, keepdims=True)
            e = jnp.where(valid, jnp.exp(s - m), 0.0)
            l = jnp.sum(e, axis=0, keepdims=True)
            p = e * jnp.where(l > 0.0, 1.0 / l, 0.0)
            psum = psum + p
            ocmpt_ref[0, rows, :] = _dot(vct, p.astype(BF16))
        imp = _dot(ovl_ref[...], psum, HIGHEST)
        v = jnp.where(forced, BIG, jnp.where(future, -BIG, imp))
        n_t = n_sel // SUBLANES
        vt = [v[a * SUBLANES:(a + 1) * SUBLANES] for a in range(n_t)]
        rank = [jnp.zeros((SUBLANES, tq), F32) for _ in range(n_t)]
        for jp in range(n_sel):
            row = jnp.broadcast_to(v[jp:jp + 1, :], (SUBLANES, tq))
            for a in range(n_t):
                if a > jp // SUBLANES:
                    hit = jnp.where(row >= vt[a], 1.0, 0.0)
                elif a < jp // SUBLANES:
                    hit = jnp.where(row > vt[a], 1.0, 0.0)
                else:
                    hit = jnp.where(jl > jp % SUBLANES, jnp.where(row >= vt[a], 1.0, 0.0),
                                    jnp.where(row > vt[a], 1.0, 0.0))
                rank[a] = rank[a] + hit
        rank = jnp.concatenate(rank, axis=0)
        negm_ref[0, hk] = jnp.where(rank < float(n_top), 0.0, NEG).astype(BF16)


def _cmp_attn(qt, kc, vct, pc, tq):
    bsz, d_attn, seq = qt.shape
    n_chunk = kc.shape[2]
    n_sel = seq // SEL_BLOCK
    n_top = min(SEL_TOPK, n_sel)
    n_cmp = (seq - CMP_BLOCK) // CMP_STRIDE + 1
    assert n_sel % SUBLANES == 0 and tq % CMP_STRIDE == 0 and (tq // CMP_STRIDE) % SUBLANES == 0
    cs = np.arange(n_chunk)[:, None] * CMP_STRIDE
    ss = np.arange(n_sel)[None, :] * SEL_BLOCK
    ovl = np.clip(np.minimum(cs + CMP_BLOCK, ss + SEL_BLOCK) - np.maximum(cs, ss), 0, None) / CMP_BLOCK
    ovl[n_cmp:] = 0.0
    ovl_t = jnp.asarray(ovl.T, dtype=F32)
    kern = functools.partial(_cmp_attn_kernel, tq=tq, n_sel=n_sel, n_top=n_top)
    return pl.pallas_call(
        kern,
        grid=(bsz, seq // tq),
        in_specs=[pl.BlockSpec((1, d_attn, tq), lambda b, i: (b, 0, i)),
                  pl.BlockSpec((1, N_KV_HEADS, n_chunk, HEAD_DIM), lambda b, i: (b, 0, 0, 0)),
                  pl.BlockSpec((1, N_KV_HEADS, HEAD_DIM, n_chunk), lambda b, i: (b, 0, 0, 0)),
                  pl.BlockSpec(pc.shape, lambda b, i: (0, 0, 0)),
                  pl.BlockSpec((n_sel, n_chunk), lambda b, i: (0, 0))],
        out_specs=[pl.BlockSpec((1, d_attn, tq), lambda b, i: (b, 0, i)),
                   pl.BlockSpec((1, N_KV_HEADS, n_sel, tq), lambda b, i: (b, 0, 0, i))],
        out_shape=[jax.ShapeDtypeStruct((bsz, d_attn, seq), F32),
                   jax.ShapeDtypeStruct((bsz, N_KV_HEADS, n_sel, seq), BF16)],
        scratch_shapes=[pltpu.VMEM((CMP_NEAR_BACK + n_chunk, tq), F32)],
        compiler_params=_cparams("parallel", "parallel"),
        name="cmp_attn",
    )(qt, kc, vct, pc, ovl_t)


def _sw_attn_kernel(qt_ref, negm_ref, ks_ref, vst_ref, kw_ref, vwt_ref, ocmpt_ref, gt_ref,
                    pd_ref, p1_ref, wold_ref, eye_ref, o_ref,
                    qa_ref, qw_ref, acc_s, acc_w, acc0_s, acc0_w, *, tq, n_win_tiles):
    qi = pl.program_id(1)
    gates = _sigmoid(gt_ref[0])

    def span(kt, n=1):
        return pl.ds(pl.multiple_of(kt * (n * ATT_TILE), n * ATT_TILE), n * ATT_TILE)

    def col_max(s):
        return jnp.max(s, axis=0, keepdims=True)

    def exists(back):
        return jnp.where(qi >= back, 0.0, NEG)

    def near_bias(ref, hk):
        return jnp.concatenate([ref[hd] for hd in range(hk * GQA, (hk + 1) * GQA)], axis=1)

    def first_tile(s, v, acc0, hk):
        m = col_max(s)
        acc0[hk] = _dot(v, jnp.exp(s - m).astype(BF16))
        return m

    def update_exact(m, rise, s, v, acc, hk):
        m_new = jnp.maximum(m, col_max(s))
        acc[hk] = jnp.exp(m - m_new) * acc[hk] + _dot(v, jnp.exp(s - m_new).astype(BF16))
        return m_new, rise

    def update_stale(m, rise, s, v, acc, hk):
        cmax = col_max(s)
        m_new = jnp.maximum(m, cmax)
        acc[hk] = (acc[hk] + _dot(v, jnp.exp(s - m).astype(BF16))) * jnp.exp(m - m_new)
        return m_new, jnp.maximum(rise, cmax - m)

    w_old = jnp.concatenate([wold_ref[...]] * GQA, axis=1)
    n_far = jnp.maximum(qi - 1, 0)
    kt1 = jnp.maximum(qi - 1, 0)
    k_odd = jnp.maximum(n_far - 1, 0)
    odd = jnp.where(n_far % 2 == 1, 0.0, NEG)

    m0 = []
    for hk in range(N_KV_HEADS):
        heads = range(hk * GQA, (hk + 1) * GQA)
        q4 = jnp.concatenate([qt_ref[0, hd * HEAD_DIM:(hd + 1) * HEAD_DIM, :] for hd in heads], axis=1)
        nm = negm_ref[0, hk]
        if nm.shape[0] < LANES - HEAD_DIM:
            nm = jnp.concatenate([nm, jnp.zeros((LANES - HEAD_DIM - nm.shape[0], tq), BF16)], axis=0)
        qa_ref[hk] = jnp.concatenate([q4, jnp.concatenate([nm] * GQA, axis=1)], axis=0)
        qw_ref[hk] = jnp.concatenate([q4, jnp.zeros_like(q4)], axis=0)
        pd4 = near_bias(pd_ref, hk)
        m0.append(first_tile(_dot(ks_ref[0, hk, span(qi), :], qa_ref[hk]) + pd4,
                             vst_ref[0, hk, :, span(qi)], acc0_s, hk))
        m0.append(first_tile(_dot(kw_ref[0, hk, span(qi), :], qw_ref[hk]) + pd4,
                             vwt_ref[0, hk, :, span(qi)], acc0_w, hk))

    def older_tiles(update):
        rise = jnp.zeros((1, GQA * tq), F32)
        m_sel = []
        for hk in range(N_KV_HEADS):
            acc_s[hk] = acc0_s[hk]
            acc_w[hk] = acc0_w[hk]
            p14 = near_bias(p1_ref, hk)
            m, rise = update(m0[2 * hk], rise, _dot(ks_ref[0, hk, span(kt1), :], qa_ref[hk]) + (p14 + exists(1)),
                             vst_ref[0, hk, :, span(kt1)], acc_s, hk)
            m_sel.append(m)
            m = m0[2 * hk + 1]
            for back in range(1, n_win_tiles + 1):
                kt = jnp.maximum(qi - back, 0)
                s = _dot(kw_ref[0, hk, span(kt), :], qw_ref[hk])
                if back == 1:
                    s = s + (p14 + exists(back))
                elif back == n_win_tiles:
                    s = s + (w_old + exists(back))
                else:
                    s = s + exists(back)
                m, rise = update(m, rise, s, vwt_ref[0, hk, :, span(kt)], acc_w, hk)

        def far(j, carry):
            ms, rise = list(carry[:-1]), carry[-1]
            for hk in range(N_KV_HEADS):
                ms[hk], rise = update(ms[hk], rise, _dot(ks_ref[0, hk, span(j, 2), :], qa_ref[hk]),
                                      vst_ref[0, hk, :, span(j, 2)], acc_s, hk)
            return (*ms, rise)

        carry = lax.fori_loop(0, n_far // 2, far, (*m_sel, rise))
        rise = carry[-1]
        for hk in range(N_KV_HEADS):
            _, rise = update(carry[hk], rise, _dot(ks_ref[0, hk, span(k_odd), :], qa_ref[hk]) + odd,
                             vst_ref[0, hk, :, span(k_odd)], acc_s, hk)
        return rise

    rise = older_tiles(update_stale)

    @pl.when(jnp.max(rise) > STALE_MAX_SLACK)
    def _():
        older_tiles(update_exact)

    ys = []
    for hk in range(N_KV_HEADS):
        o_sel = acc_s[hk, :HEAD_DIM] * (1.0 / acc_s[hk, HEAD_DIM:HEAD_DIM + 1])
        o_win = acc_w[hk, :HEAD_DIM] * (1.0 / acc_w[hk, HEAD_DIM:HEAD_DIM + 1])
        for g in range(GQA):
            hd = hk * GQA + g
            cols = slice(g * tq, (g + 1) * tq)
            ys.append(gates[3 * hd:3 * hd + 1] * ocmpt_ref[0, hd * HEAD_DIM:(hd + 1) * HEAD_DIM, :]
                      + gates[3 * hd + 1:3 * hd + 2] * o_sel[:, cols]
                      + gates[3 * hd + 2:3 * hd + 3] * o_win[:, cols])
    yt = jnp.concatenate(ys, axis=0).astype(BF16)
    o_ref[0] = _dot_nt(eye_ref[...], yt).astype(BF16)


def _sw_attn(qt, negm, ks, vst, kw, vwt, ocmpt, gt, pd, p1, w_old, tq):
    bsz, d_attn, seq = qt.shape
    n_sel = negm.shape[2]
    assert tq == ATT_TILE and WINDOW % ATT_TILE == 0 and n_sel <= LANES - HEAD_DIM
    eye = jnp.eye(tq, dtype=BF16)
    kern = functools.partial(_sw_attn_kernel, tq=tq, n_win_tiles=WINDOW // ATT_TILE)
    k_spec = pl.BlockSpec((1, N_KV_HEADS, seq, LANES), lambda b, i: (b, 0, 0, 0))
    vt_spec = pl.BlockSpec((1, N_KV_HEADS, V_ROWS, seq), lambda b, i: (b, 0, 0, 0))
    qt_spec = pl.BlockSpec((1, d_attn, tq), lambda b, i: (b, 0, i))
    full = lambda shape: pl.BlockSpec(shape, lambda b, i: (0,) * len(shape))
    return pl.pallas_call(
        kern,
        grid=(bsz, seq // tq),
        in_specs=[qt_spec,
                  pl.BlockSpec((1, N_KV_HEADS, n_sel, tq), lambda b, i: (b, 0, 0, i)),
                  k_spec, vt_spec, k_spec, vt_spec,
                  qt_spec,
                  pl.BlockSpec((1, gt.shape[1], tq), lambda b, i: (b, 0, i)),
                  full(pd.shape), full(p1.shape), full(w_old.shape), full(eye.shape)],
        out_specs=pl.BlockSpec((1, tq, d_attn), lambda b, i: (b, i, 0)),
        out_shape=jax.ShapeDtypeStruct((bsz, seq, d_attn), BF16),
        scratch_shapes=[pltpu.VMEM((N_KV_HEADS, LANES, GQA * tq), BF16)] * 2
                       + [pltpu.VMEM((N_KV_HEADS, V_ROWS, GQA * tq), F32)] * 4,
        compiler_params=_cparams("parallel", "parallel"),
        name="sel_win_attn",
    )(qt, negm, ks, vst, kw, vwt, ocmpt, gt, pd, p1, w_old, eye)


def _outproj_kernel(x_ref, ys_ref, ya_ref, ws_ref, wa_ref, o_ref):
    o_ref[0] = x_ref[0] + _dot(ys_ref[...], ws_ref[...]) + _dot(ya_ref[0], wa_ref[...])


def _outproj(x, y_ssm_tm, y_attn, w_out, tm):
    bsz, seq, d = x.shape
    d_ssm = y_ssm_tm.shape[1] // bsz
    d_attn = y_attn.shape[2]
    wb = w_out.astype(BF16)
    ws, wa = wb[:d_ssm], wb[d_ssm:]
    full = lambda shape: pl.BlockSpec(shape, lambda b, i: (0,) * len(shape))
    return pl.pallas_call(
        _outproj_kernel,
        grid=(bsz, seq // tm),
        in_specs=[pl.BlockSpec((1, tm, d), lambda b, i: (b, i, 0)),
                  pl.BlockSpec((tm, d_ssm), lambda b, i: (i, b)),
                  pl.BlockSpec((1, tm, d_attn), lambda b, i: (b, i, 0)),
                  full(ws.shape), full(wa.shape)],
        out_specs=pl.BlockSpec((1, tm, d), lambda b, i: (b, i, 0)),
        out_shape=jax.ShapeDtypeStruct((bsz, seq, d), F32),
        compiler_params=_cparams("parallel", "parallel"),
        name="outproj",
    )(x, y_ssm_tm, y_attn, ws, wa)


def _router_kernel(x_ref, g_ref, wr_ref, br_ref, h_ref, route_ref):
    x = x_ref[...]
    ms = jnp.mean(x * x, axis=-1, keepdims=True)
    h = x * lax.rsqrt(ms + EPS) * g_ref[...]
    h_ref[...] = h.astype(BF16)
    logits = _dot(h, wr_ref[...], HIGHEST) + br_ref[...]
    lane = lax.broadcasted_iota(jnp.int32, logits.shape, 1)
    is_grp = (lane >= N_EXPERTS) & (lane < N_EXPERTS + N_GROUPS)
    lg = jnp.where(is_grp, logits, NEG)
    mg = jnp.max(lg, axis=-1, keepdims=True)
    g_val = 1.0 / jnp.sum(jnp.where(is_grp, jnp.exp(lg - mg), 0.0), axis=-1, keepdims=True)
    g_idx = jnp.min(jnp.where(is_grp & (lg == mg), lane - N_EXPERTS, N_GROUPS), axis=-1, keepdims=True)
    in_grp = (lane >= g_idx * EXPERTS_PER_GROUP) & (lane < (g_idx + 1) * EXPERTS_PER_GROUP)
    le = jnp.where(in_grp, logits, NEG)
    me = jnp.max(le, axis=-1, keepdims=True)
    ee = jnp.where(in_grp, jnp.exp(le - me), 0.0)
    pe = ee / jnp.sum(ee, axis=-1, keepdims=True)
    p1 = jnp.max(pe, axis=-1, keepdims=True)
    i1 = jnp.min(jnp.where(in_grp & (pe == p1), lane, LANES), axis=-1, keepdims=True)
    rest = in_grp & (lane != i1)
    pr = jnp.where(rest, pe, -1.0)
    p2 = jnp.max(pr, axis=-1, keepdims=True)
    i2 = jnp.min(jnp.where(rest & (pr == p2), lane, LANES), axis=-1, keepdims=True)
    tot = p1 + p2
    route_ref[...] = jnp.where(lane == 0, i1.astype(F32),
                               jnp.where(lane == 1, i2.astype(F32),
                                         jnp.where(lane == 2, g_val * (p1 / tot),
                                                   jnp.where(lane == 3, g_val * (p2 / tot), 0.0))))


def _router(x2d, g, rg_w, rg_b, re_w, re_b, tm):
    tok, d = x2d.shape
    we = jnp.transpose(re_w, (1, 0, 2)).reshape(d, N_EXPERTS)
    wr = jnp.pad(jnp.concatenate([we, rg_w], axis=1), ((0, 0), (0, LANES - N_EXPERTS - N_GROUPS)))
    br = jnp.pad(jnp.concatenate([re_b.reshape(-1), rg_b]), (0, LANES - N_EXPERTS - N_GROUPS)).reshape(1, LANES)
    full = lambda shape: pl.BlockSpec(shape, lambda i: (0,) * len(shape))
    return pl.pallas_call(
        _router_kernel,
        grid=(tok // tm,),
        in_specs=[pl.BlockSpec((tm, d), lambda i: (i, 0)), full((1, d)), full(wr.shape), full(br.shape)],
        out_specs=[pl.BlockSpec((tm, d), lambda i: (i, 0)), pl.BlockSpec((tm, LANES), lambda i: (i, 0))],
        out_shape=[jax.ShapeDtypeStruct((tok, d), BF16), jax.ShapeDtypeStruct((tok, LANES), F32)],
        compiler_params=_cparams("parallel"),
        name="router",
    )(x2d, g.reshape(1, d), wr, br)


MOE_ROW_TILE = 256


def _dispatch_plan(route, tok):
    n_pairs = 2 * tok
    pe = jnp.concatenate([route[:, 0], route[:, 1]]).astype(jnp.int32)
    pw = jnp.concatenate([route[:, 2], route[:, 3]])
    ptok = jnp.concatenate([jnp.arange(tok, dtype=jnp.int32)] * 2)
    onehot = (pe[:, None] == jnp.arange(N_EXPERTS, dtype=jnp.int32)[None, :]).astype(jnp.int32)
    csum = jnp.cumsum(onehot, axis=0)
    counts = csum[-1]
    cend = jnp.cumsum(counts)
    cstart = cend - counts
    dest = jnp.sum(onehot * (cstart[None, :] + csum - 1), axis=1)
    _, tok_sorted, w_sorted = lax.sort((pe, ptok, pw), num_keys=1, is_stable=True)
    first_tile = cstart // MOE_ROW_TILE
    last_tile = (cend - 1) // MOE_ROW_TILE
    n_vis = jnp.where(counts > 0, last_tile - first_tile + 1, 0)
    vis_end = jnp.cumsum(n_vis)
    n_visits = vis_end[-1]
    max_visits = n_pairs // MOE_ROW_TILE + N_EXPERTS - 1
    v = jnp.minimum(jnp.arange(max_visits, dtype=jnp.int32), n_visits - 1)
    ve = jnp.searchsorted(vis_end, v, side='right').astype(jnp.int32)
    vt = first_tile[ve] + (v - (vis_end[ve] - n_vis[ve]))
    vfirst = jnp.concatenate([jnp.ones((1,), jnp.int32), (vt[1:] != vt[:-1]).astype(jnp.int32)])
    meta = jnp.stack([ve, vt.astype(jnp.int32), vfirst, cstart[ve], cend[ve]]).astype(jnp.int32)
    return dest, tok_sorted, w_sorted, meta, n_visits.reshape(1).astype(jnp.int32)


def _moe_kernel(meta_ref, nv_ref, x_ref, w_ref, wgu_ref, wd_ref, o_ref):
    v = pl.program_id(0)

    @pl.when(v < nv_ref[0])
    def _():
        f = wd_ref.shape[1]
        row = meta_ref[1, v] * MOE_ROW_TILE + lax.broadcasted_iota(jnp.int32, (MOE_ROW_TILE, 1), 0)
        mine = (row >= meta_ref[3, v]) & (row < meta_ref[4, v])
        gu = _dot(x_ref[...], wgu_ref[0])
        a = gu[:, :f]
        hid = jnp.where(mine, (a * _sigmoid(a)) * gu[:, f:] * w_ref[...], 0.0)
        y = _dot(hid.astype(BF16), wd_ref[0])

        @pl.when(meta_ref[2, v] == 1)
        def _():
            o_ref[...] = y

        @pl.when(meta_ref[2, v] == 0)
        def _():
            o_ref[...] += y


def _moe_experts(x_sorted, w_sorted, meta, n_visits, w_gate, w_up, w_down):
    rows, d = x_sorted.shape
    f = w_gate.shape[-1]
    wgu = jnp.concatenate([w_gate.reshape(N_EXPERTS, d, f), w_up.reshape(N_EXPERTS, d, f)], axis=-1).astype(BF16)
    wd = w_down.reshape(N_EXPERTS, f, d).astype(BF16)
    grid_spec = pltpu.PrefetchScalarGridSpec(
        num_scalar_prefetch=2,
        grid=(meta.shape[1],),
        in_specs=[pl.BlockSpec((MOE_ROW_TILE, d), lambda v, m, n: (m[1, v], 0)),
                  pl.BlockSpec((MOE_ROW_TILE, 1), lambda v, m, n: (m[1, v], 0)),
                  pl.BlockSpec((1, d, 2 * f), lambda v, m, n: (m[0, v], 0, 0)),
                  pl.BlockSpec((1, f, d), lambda v, m, n: (m[0, v], 0, 0))],
        out_specs=pl.BlockSpec((MOE_ROW_TILE, d), lambda v, m, n: (m[1, v], 0)),
    )
    return pl.pallas_call(
        _moe_kernel,
        grid_spec=grid_spec,
        out_shape=jax.ShapeDtypeStruct((rows, d), F32),
        compiler_params=_cparams("arbitrary"),
        name="moe_experts",
    )(meta, n_visits, x_sorted, w_sorted.reshape(rows, 1), wgu, wd)


def _combine_kernel(x_ref, y1_ref, y2_ref, g_ref, o_ref, *, final_norm):
    x = x_ref[...] + y1_ref[...] + y2_ref[...]
    if final_norm:
        ms = jnp.mean(x * x, axis=-1, keepdims=True)
        x = x * lax.rsqrt(ms + EPS) * g_ref[...]
    o_ref[...] = x


def _combine(x2d, y1, y2, g, final_norm, tm):
    tok, d = x2d.shape
    blk = pl.BlockSpec((tm, d), lambda i: (i, 0))
    return pl.pallas_call(
        functools.partial(_combine_kernel, final_norm=final_norm),
        grid=(tok // tm,),
        in_specs=[blk, blk, blk, pl.BlockSpec((1, d), lambda i: (0, 0))],
        out_specs=blk,
        out_shape=jax.ShapeDtypeStruct((tok, d), F32),
        compiler_params=_cparams("parallel"),
        name="moe_combine",
    )(x2d, y1, y2, g.reshape(1, d))


def _moe(h, route, x2d, w_gate, w_up, w_down, g_final, final_norm, tm):
    tok = x2d.shape[0]
    dest, tok_sorted, w_sorted, meta, n_visits = _dispatch_plan(route, tok)
    x_sorted = jnp.take(h, tok_sorted, axis=0)
    y_sorted = _moe_experts(x_sorted, w_sorted, meta, n_visits, w_gate, w_up, w_down)
    y1 = jnp.take(y_sorted, dest[:tok], axis=0)
    y2 = jnp.take(y_sorted, dest[tok:], axis=0)
    return _combine(x2d, y1, y2, g_final, final_norm, tm)


def kernel(x, norm1_g, norm2_g, final_g, w_in, w_out, ssm_a_re, ssm_a_im, ssm_b_re, ssm_b_im, ssm_c_re, ssm_c_im, ssm_d, ssm_log_dt, ssm_w_glu, cmp_pos, cmp_w1, cmp_w2, rel_bias, router_g_w, router_g_b, router_e_w, router_e_b, exp_w_gate, exp_w_up, exp_w_down):
    bsz, seq, d = x.shape
    depth = norm1_g.shape[0]
    tm = min(512, seq)
    tok_tile = min(1024, bsz * seq)
    pd, p1, w_old, pc = _bias_tables(rel_bias, seq)
    for l in range(depth):
        u_tm, k_cmp, v_cmp, ks, kw, qt, vst, vwt, gt = _inproj(x, norm1_g[l], w_in[l], tm)
        y_ssm = _s5(u_tm, bsz, ssm_a_re[l], ssm_a_im[l], ssm_b_re[l], ssm_b_im[l], ssm_c_re[l],
                    ssm_c_im[l], ssm_d[l].reshape(-1), ssm_log_dt[l], ssm_w_glu[l], tc=min(32, seq))
        kc, vct = _compress(k_cmp, v_cmp, cmp_pos[l], cmp_w1[l], cmp_w2[l])
        ocmpt, negm = _cmp_attn(qt, kc, vct, pc, ATT_TILE)
        y_attn = _sw_attn(qt, negm, ks, vst, kw, vwt, ocmpt, gt, pd, p1, w_old, ATT_TILE)
        x = _outproj(x, y_ssm, y_attn, w_out[l], tm)
        x2d = x.reshape(bsz * seq, d)
        h, route = _router(x2d, norm2_g[l], router_g_w[l], router_g_b[l], router_e_w[l], router_e_b[l], tok_tile)
        x = _moe(h, route, x2d, exp_w_gate[l], exp_w_up[l], exp_w_down[l], final_g, l == depth - 1,
                 tok_tile).reshape(bsz, seq, d)
    return x
```

```python
import functools
import math

import numpy as np
import jax
import jax.numpy as jnp
from jax import lax
from jax.experimental import pallas as pl
from jax.experimental.pallas import tpu as pltpu

F32 = jnp.float32
BF16 = jnp.bfloat16

SSM_GROUP = 16
SSM_STATE = 64
HEAD_DIM = 64
N_KV_HEADS = 2
GQA = 4
N_HEADS = N_KV_HEADS * GQA
CMP_BLOCK = 32
CMP_STRIDE = 16
SEL_BLOCK = 64
SEL_TOPK = 16
N_LOCAL_BLOCKS = 2
WINDOW = 512
N_BUCKETS = 32
MAX_DISTANCE = 128
N_GROUPS = 4
EXPERTS_PER_GROUP = 8
N_EXPERTS = N_GROUPS * EXPERTS_PER_GROUP
EPS = 1e-6
NEG = -1e30
BIG = 1e9

LANES = 128
SUBLANES = 8
ATT_TILE = 128
V_ROWS = HEAD_DIM + 16
SSM_OCT = 8
VMEM_LIMIT = 56 * 1024 * 1024

HIGHEST = lax.Precision.HIGHEST


def _cparams(*sem):
    return pltpu.CompilerParams(dimension_semantics=sem, vmem_limit_bytes=VMEM_LIMIT)


def _dot(a, b, precision=None):
    return jnp.dot(a, b, preferred_element_type=F32, precision=precision)


def _dot_nt(a, b, precision=None):
    return lax.dot_general(a, b, (((1,), (1,)), ((), ())),
                           preferred_element_type=F32, precision=precision)


def _gelu_tanh(x):
    return 0.5 * x * (1.0 + jnp.tanh(math.sqrt(2.0 / math.pi) * (x + 0.044715 * (x * x * x))))


def _sigmoid(x):
    return 1.0 / (1.0 + jnp.exp(-x))


def _t5_bucket_np(dist):
    n = np.maximum(dist, 0)
    max_exact = N_BUCKETS // 2
    nf = np.maximum(n, 1).astype(np.float32)
    large = max_exact + (np.log(nf / np.float32(max_exact))
                         / np.float32(math.log(MAX_DISTANCE / max_exact))
                         * np.float32(N_BUCKETS - max_exact)).astype(np.int32)
    large = np.minimum(large, N_BUCKETS - 1)
    return np.where(n < max_exact, n, large).astype(np.int32)


def _inproj_kernel(x_ref, g_ref, wu_ref, wc_ref, wk_ref, wqt_ref, wvt_ref, wgt_ref,
                   u_ref, kc_ref, vc_ref, ks_ref, kw_ref, qt_ref, vst_ref, vwt_ref, gt_ref, *, tm):
    x = x_ref[0]
    ms = jnp.mean(x * x, axis=-1, keepdims=True)
    h = (x * lax.rsqrt(ms + EPS) * g_ref[...]).astype(BF16)
    u_ref[...] = _dot(h, wu_ref[...])
    dkv = N_KV_HEADS * HEAD_DIM
    c = _dot(h, wc_ref[...])
    kc_ref[0] = c[:, :dkv]
    vc_ref[0] = c[:, dkv:]
    ka = _dot(h, wk_ref[...])
    blk = (pl.program_id(1) * tm + lax.broadcasted_iota(jnp.int32, (tm, LANES), 0)) // SEL_BLOCK
    onehot = jnp.where(lax.broadcasted_iota(jnp.int32, (tm, LANES), 1) - HEAD_DIM == blk, 1.0, 0.0)
    for hk in range(N_KV_HEADS):
        ks_ref[0, hk] = (ka[:, hk * LANES:(hk + 1) * LANES] + onehot).astype(BF16)
        kw_ref[0, hk] = ka[:, (N_KV_HEADS + hk) * LANES:(N_KV_HEADS + hk + 1) * LANES].astype(BF16)
    qt_ref[0] = (_dot_nt(wqt_ref[...], h) * (HEAD_DIM ** -0.5)).astype(BF16)
    vt = _dot_nt(wvt_ref[...], h).astype(BF16)
    ones_rows = jnp.where(lax.broadcasted_iota(jnp.int32, (V_ROWS - HEAD_DIM, tm), 0) == 0, 1.0, 0.0).astype(BF16)
    for hk in range(N_KV_HEADS):
        vst_ref[0, hk] = jnp.concatenate([vt[hk * HEAD_DIM:(hk + 1) * HEAD_DIM], ones_rows], axis=0)
        vwt_ref[0, hk] = jnp.concatenate([vt[dkv + hk * HEAD_DIM:dkv + (hk + 1) * HEAD_DIM], ones_rows], axis=0)
    gt_ref[0] = _dot_nt(wgt_ref[...], h)


def _inproj(x, g, w_in, tm):
    bsz, seq, d = x.shape
    d_ssm = d // 2
    d_attn = d - d_ssm
    dkv = N_KV_HEADS * HEAD_DIM
    o_kv = d_ssm + d_attn
    o_g = o_kv + 6 * dkv
    assert seq // SEL_BLOCK <= LANES - HEAD_DIM
    wb = w_in.astype(BF16)
    wu = wb[:, :d_ssm]
    wq = wb[:, d_ssm:o_kv]
    k_cmp, v_cmp, k_sel, v_sel, k_win, v_win = [wb[:, o_kv + i * dkv:o_kv + (i + 1) * dkv] for i in range(6)]
    wc = jnp.concatenate([k_cmp, v_cmp], axis=1)

    def pad_heads(w):
        w = w.reshape(d, N_KV_HEADS, HEAD_DIM)
        return jnp.pad(w, ((0, 0), (0, 0), (0, LANES - HEAD_DIM))).reshape(d, N_KV_HEADS * LANES)

    wk = jnp.concatenate([pad_heads(k_sel), pad_heads(k_win)], axis=1)
    wqt = wq.T
    wvt = jnp.concatenate([v_sel, v_win], axis=1).T
    n_gate_rows = 4 * SUBLANES
    wgt = jnp.pad(wb[:, o_g:], ((0, 0), (0, n_gate_rows - 3 * N_HEADS))).T
    full = lambda shape: pl.BlockSpec(shape, lambda b, i: (0,) * len(shape))
    k_spec = pl.BlockSpec((1, N_KV_HEADS, tm, LANES), lambda b, i: (b, 0, i, 0))
    vt_spec = pl.BlockSpec((1, N_KV_HEADS, V_ROWS, tm), lambda b, i: (b, 0, 0, i))
    k_shape = jax.ShapeDtypeStruct((bsz, N_KV_HEADS, seq, LANES), BF16)
    vt_shape = jax.ShapeDtypeStruct((bsz, N_KV_HEADS, V_ROWS, seq), BF16)
    return pl.pallas_call(
        functools.partial(_inproj_kernel, tm=tm),
        grid=(bsz, seq // tm),
        in_specs=[pl.BlockSpec((1, tm, d), lambda b, i: (b, i, 0)),
                  full((1, d)), full(wu.shape), full(wc.shape), full(wk.shape),
                  full(wqt.shape), full(wvt.shape), full(wgt.shape)],
        out_specs=[pl.BlockSpec((tm, d_ssm), lambda b, i: (i, b)),
                   pl.BlockSpec((1, tm, dkv), lambda b, i: (b, i, 0)),
                   pl.BlockSpec((1, tm, dkv), lambda b, i: (b, i, 0)),
                   k_spec, k_spec,
                   pl.BlockSpec((1, d_attn, tm), lambda b, i: (b, 0, i)),
                   vt_spec, vt_spec,
                   pl.BlockSpec((1, n_gate_rows, tm), lambda b, i: (b, 0, i))],
        out_shape=[jax.ShapeDtypeStruct((seq, bsz * d_ssm), F32),
                   jax.ShapeDtypeStruct((bsz, seq, dkv), F32),
                   jax.ShapeDtypeStruct((bsz, seq, dkv), F32),
                   k_shape, k_shape,
                   jax.ShapeDtypeStruct((bsz, d_attn, seq), BF16),
                   vt_shape, vt_shape,
                   jax.ShapeDtypeStruct((bsz, n_gate_rows, seq), F32)],
        compiler_params=_cparams("parallel", "parallel"),
        name="inproj",
    )(x, g.reshape(1, d), wu, wc, wk, wqt, wvt, wgt)


def _s5_kernel(u_ref, bmat_ref, cmat_ref, lam_ref, d_ref, wglu_ref, y_ref, bu_ref, st_ref,
               *, tc, bsz, n_oct, oct_c, oct_s, lane_chunk):
    rows = tc * bsz
    d_ssm = n_oct * oct_c
    n_state = n_oct * oct_s

    @pl.when(pl.program_id(0) == 0)
    def _():
        st_ref[...] = jnp.zeros_like(st_ref)

    u = u_ref[...].reshape(rows, d_ssm)
    ub = u.astype(BF16)
    for o in range(n_oct):
        r = _dot(ub[:, o * oct_c:(o + 1) * oct_c], bmat_ref[o])
        bu_ref[:, o * oct_s:(o + 1) * oct_s] = r[:, :oct_s]
        bu_ref[:, n_state + o * oct_s:n_state + (o + 1) * oct_s] = r[:, oct_s:]

    for c0 in range(0, n_state, lane_chunk):
        re = slice(c0, c0 + lane_chunk)
        im = slice(n_state + c0, n_state + c0 + lane_chunk)

        def step(t, carry, re=re, im=im):
            sr, si = carry
            row = pl.multiple_of(t * bsz, bsz)
            lr = lam_ref[:, re]
            li = lam_ref[:, im]
            nr = lr * sr - li * si + bu_ref[pl.ds(row, bsz), re]
            ni = lr * si + li * sr + bu_ref[pl.ds(row, bsz), im]
            bu_ref[pl.ds(row, bsz), re] = nr
            bu_ref[pl.ds(row, bsz), im] = ni
            return nr, ni

        sr, si = lax.fori_loop(0, tc, step, (st_ref[:, re], st_ref[:, im]))
        st_ref[:, re] = sr
        st_ref[:, im] = si

    ys = []
    for o in range(n_oct):
        s_re = bu_ref[:, o * oct_s:(o + 1) * oct_s].astype(BF16)
        s_im = bu_ref[:, n_state + o * oct_s:n_state + (o + 1) * oct_s].astype(BF16)
        ys.append(_dot(s_re, cmat_ref[o, :oct_s]) + _dot(s_im, cmat_ref[o, oct_s:]))
    y = jnp.concatenate(ys, axis=1) + d_ref[...] * u
    y = _gelu_tanh(y).astype(BF16)
    z = _dot(y, wglu_ref[...])
    out = z[:, :d_ssm] * _sigmoid(z[:, d_ssm:])
    y_ref[...] = out.astype(BF16).reshape(tc, bsz, d_ssm)


def _s5_params(a_re, a_im, b_re, b_im, c_re, c_im, log_dt, bsz):
    g, p = a_re.shape
    h = b_re.shape[-1]
    n_oct = g // SSM_OCT
    dt = jnp.exp(log_dt)[:, None]
    mag = jnp.exp(a_re * dt)
    lr = mag * jnp.cos(a_im * dt)
    li = mag * jnp.sin(a_im * dt)
    den = a_re * a_re + a_im * a_im
    kr = ((lr - 1.0) * a_re + li * a_im) / den
    ki = (li * a_re - (lr - 1.0) * a_im) / den
    bb_re = kr[..., None] * b_re - ki[..., None] * b_im
    bb_im = kr[..., None] * b_im + ki[..., None] * b_re
    eye = jnp.eye(SSM_OCT, dtype=F32)

    def bd_in(m):
        m = m.reshape(n_oct, SSM_OCT, p, h)
        return jnp.einsum('ogph,gk->oghkp', m, eye).reshape(n_oct, SSM_OCT * h, SSM_OCT * p)

    def bd_out(m):
        m = m.reshape(n_oct, SSM_OCT, h, p)
        return jnp.einsum('oghp,gk->ogpkh', m, eye).reshape(n_oct, SSM_OCT * p, SSM_OCT * h)

    bmat = jnp.concatenate([bd_in(bb_re), bd_in(bb_im)], axis=-1).astype(BF16)
    cmat = jnp.concatenate([bd_out(c_re), bd_out(-c_im)], axis=1).astype(BF16)
    lam = jnp.concatenate([lr.reshape(1, g * p), li.reshape(1, g * p)], axis=1)
    lam = jnp.broadcast_to(lam, (bsz, 2 * g * p))
    return bmat, cmat, lam


def _s5(u_tm, bsz, a_re, a_im, b_re, b_im, c_re, c_im, d, log_dt, w_glu, tc):
    seq = u_tm.shape[0]
    d_ssm = u_tm.shape[1] // bsz
    g, p = a_re.shape
    n_oct = g // SSM_OCT
    oct_c = SSM_OCT * SSM_GROUP
    oct_s = SSM_OCT * p
    bmat, cmat, lam = _s5_params(a_re, a_im, b_re, b_im, c_re, c_im, log_dt, bsz)
    u3 = u_tm.reshape(seq, bsz, d_ssm)
    full = lambda shape: pl.BlockSpec(shape, lambda i: (0,) * len(shape))
    kern = functools.partial(_s5_kernel, tc=tc, bsz=bsz, n_oct=n_oct, oct_c=oct_c, oct_s=oct_s,
                             lane_chunk=min(1024, g * p))
    y = pl.pallas_call(
        kern,
        grid=(seq // tc,),
        in_specs=[pl.BlockSpec((tc, bsz, d_ssm), lambda i: (i, 0, 0)),
                  full(bmat.shape), full(cmat.shape), full(lam.shape),
                  full((1, d_ssm)), full(w_glu.shape)],
        out_specs=pl.BlockSpec((tc, bsz, d_ssm), lambda i: (i, 0, 0)),
        out_shape=jax.ShapeDtypeStruct((seq, bsz, d_ssm), BF16),
        scratch_shapes=[pltpu.VMEM((tc * bsz, 2 * g * p), F32),
                        pltpu.VMEM((bsz, 2 * g * p), F32)],
        compiler_params=_cparams("arbitrary"),
        name="s5_mixer",
    )(u3, bmat, cmat, lam, d.reshape(1, d_ssm), w_glu.astype(BF16))
    return y.reshape(seq, bsz * d_ssm)


def _compress_kernel(k_ref, v_ref, w1a_ref, w1b_ref, posw_ref, w2_ref, w2t_ref, kc_ref, vct_ref):
    def hidden(i, src):
        x = src[0]
        a = _dot(x, w1a_ref[i], HIGHEST)
        b = _dot(x, w1b_ref[i], HIGHEST)
        return _gelu_tanh(a + pltpu.roll(b, a.shape[0] - 1, 0) + posw_ref[i])

    kc = _dot(hidden(0, k_ref), w2_ref[...], HIGHEST).astype(BF16)
    vct = _dot_nt(w2t_ref[...], hidden(1, v_ref), HIGHEST).astype(BF16)
    for hk in range(N_KV_HEADS):
        kc_ref[0, hk] = kc[:, hk * HEAD_DIM:(hk + 1) * HEAD_DIM]
        vct_ref[0, hk] = vct[hk * HEAD_DIM:(hk + 1) * HEAD_DIM]


def _compress(k_cmp, v_cmp, cmp_pos, cmp_w1, cmp_w2):
    bsz, seq, dkv = k_cmp.shape
    n_chunk = seq // CMP_STRIDE
    kview = k_cmp.reshape(bsz, n_chunk, CMP_STRIDE * dkv)
    vview = v_cmp.reshape(bsz, n_chunk, CMP_STRIDE * dkv)
    eye = jnp.eye(N_KV_HEADS, dtype=F32)
    w1 = jnp.einsum('ilde,hk->ilhdke', cmp_w1, eye)
    w1 = w1.reshape(2, CMP_BLOCK, dkv, dkv)
    w1a = w1[:, :CMP_STRIDE].reshape(2, CMP_STRIDE * dkv, dkv)
    w1b = w1[:, CMP_STRIDE:].reshape(2, CMP_STRIDE * dkv, dkv)
    posw = jnp.einsum('ild,ilde->ie', cmp_pos, cmp_w1, precision=HIGHEST)
    posw = jnp.tile(posw, (1, N_KV_HEADS)).reshape(2, 1, dkv)
    w2 = jnp.einsum('ief,hk->ihekf', cmp_w2, eye).reshape(2, dkv, dkv)
    full = lambda shape: pl.BlockSpec(shape, lambda b: (0,) * len(shape))
    return pl.pallas_call(
        _compress_kernel,
        grid=(bsz,),
        in_specs=[pl.BlockSpec((1, n_chunk, CMP_STRIDE * dkv), lambda b: (b, 0, 0)),
                  pl.BlockSpec((1, n_chunk, CMP_STRIDE * dkv), lambda b: (b, 0, 0)),
                  full(w1a.shape), full(w1b.shape), full(posw.shape), full((dkv, dkv)), full((dkv, dkv))],
        out_specs=[pl.BlockSpec((1, N_KV_HEADS, n_chunk, HEAD_DIM), lambda b: (b, 0, 0, 0)),
                   pl.BlockSpec((1, N_KV_HEADS, HEAD_DIM, n_chunk), lambda b: (b, 0, 0, 0))],
        out_shape=[jax.ShapeDtypeStruct((bsz, N_KV_HEADS, n_chunk, HEAD_DIM), BF16),
                   jax.ShapeDtypeStruct((bsz, N_KV_HEADS, HEAD_DIM, n_chunk), BF16)],
        compiler_params=_cparams("parallel"),
        name="compress",
    )(kview, vview, w1a, w1b, posw, w2[0], w2[1].T)


CMP_NEAR_BACK = 2 * SUBLANES
CMP_NEAR_ROWS = 3 * SUBLANES


def _bias_tables(rel_bias, seq):
    last = rel_bias[N_BUCKETS - 1]
    assert int(_t5_bucket_np(np.arange(ATT_TILE + 1, max(seq, 2 * ATT_TILE))).min()) == N_BUCKETS - 1
    s = np.arange(ATT_TILE)[:, None]
    t = np.arange(ATT_TILE)[None, :]

    def table(dist):
        return jnp.transpose(rel_bias[jnp.asarray(_t5_bucket_np(dist))] - last, (2, 0, 1))

    causal = jnp.asarray(t >= s)
    pd = jnp.where(causal, table(t - s), NEG)
    p1 = table(ATT_TILE + t - s)
    w_old = jnp.where(jnp.asarray(s > t), 0.0, NEG).astype(F32)
    m = np.arange(CMP_NEAR_ROWS)[:, None] - CMP_NEAR_BACK
    dist_c = t - (CMP_STRIDE * m + CMP_BLOCK - 1)
    assert dist_c[0].min() > ATT_TILE
    pc = jnp.where(jnp.asarray(dist_c >= 0), table(dist_c), 0.0)
    return pd, p1, w_old, pc


def _cmp_attn_kernel(qt_ref, kc_ref, vct_ref, pc_ref, ovl_ref, ocmpt_ref, negm_ref, s_scr,
                     *, tq, n_sel, n_top):
    qi = pl.program_id(1)
    n_chunk = kc_ref.shape[2]
    n_i = lax.broadcasted_iota(jnp.int32, (n_chunk, tq), 0)
    t_i = qi * tq + lax.broadcasted_iota(jnp.int32, (n_chunk, tq), 1)
    valid = n_i * CMP_STRIDE + (CMP_BLOCK - 1) <= t_i
    jrow = lax.broadcasted_iota(jnp.int32, (n_sel, tq), 0)
    blk_t = (qi * tq + lax.broadcasted_iota(jnp.int32, (n_sel, tq), 1)) // SEL_BLOCK
    forced = (jrow == 0) | ((jrow <= blk_t) & (jrow > blk_t - N_LOCAL_BLOCKS))
    future = jrow > blk_t
    jl = lax.broadcasted_iota(jnp.int32, (SUBLANES, tq), 0)
    for hd in range(N_HEADS):
        s_scr[hd, 0:CMP_NEAR_BACK, :] = jnp.zeros((CMP_NEAR_BACK, tq), F32)
    near = pl.ds(pl.multiple_of(qi * (tq // CMP_STRIDE), SUBLANES), CMP_NEAR_ROWS)
    body = slice(CMP_NEAR_BACK, CMP_NEAR_BACK + n_chunk)
    for hk in range(N_KV_HEADS):
        kc = kc_ref[0, hk]
        vct = vct_ref[0, hk]
        psum = jnp.zeros((n_chunk, tq), F32)
        for g in range(GQA):
            hd = hk * GQA + g
            rows = slice(hd * HEAD_DIM, (hd + 1) * HEAD_DIM)
            s_scr[hd, body, :] = _dot(kc, qt_ref[0, rows, :])
            s_scr[hd, near, :] = s_scr[hd, near, :] + pc_ref[hd]
            s = jnp.where(valid, s_scr[hd, body, :], NEG)
            m = jnp.max(s, axis=0, keepdims=True)
            e = jnp.where(valid, jnp.exp(s - m), 0.0)
            l = jnp.sum(e, axis=0, keepdims=True)
            p = e * jnp.where(l > 0.0, 1.0 / l, 0.0)
            psum = psum + p
            ocmpt_ref[0, rows, :] = _dot(vct, p.astype(BF16))
        imp = _dot(ovl_ref[...], psum, HIGHEST)
        v = jnp.where(forced, BIG, jnp.where(future, -BIG, imp))
        n_t = n_sel // SUBLANES
        vt = [v[a * SUBLANES:(a + 1) * SUBLANES] for a in range(n_t)]
        rank = [jnp.zeros((SUBLANES, tq), F32) for _ in range(n_t)]
        for jp in range(n_sel):
            row = jnp.broadcast_to(v[jp:jp + 1, :], (SUBLANES, tq))
            for a in range(n_t):
                if a > jp // SUBLANES:
                    hit = jnp.where(row >= vt[a], 1.0, 0.0)
                elif a < jp // SUBLANES:
                    hit = jnp.where(row > vt[a], 1.0, 0.0)
                else:
                    hit = jnp.where(jl > jp % SUBLANES, jnp.where(row >= vt[a], 1.0, 0.0),
                                    jnp.where(row > vt[a], 1.0, 0.0))
                rank[a] = rank[a] + hit
        rank = jnp.concatenate(rank, axis=0)
        negm_ref[0, hk] = jnp.where(rank < float(n_top), 0.0, NEG).astype(BF16)


def _cmp_attn(qt, kc, vct, pc, tq):
    bsz, d_attn, seq = qt.shape
    n_chunk = kc.shape[2]
    n_sel = seq // SEL_BLOCK
    n_top = min(SEL_TOPK, n_sel)
    n_cmp = (seq - CMP_BLOCK) // CMP_STRIDE + 1
    assert n_sel % SUBLANES == 0 and tq % CMP_STRIDE == 0 and (tq // CMP_STRIDE) % SUBLANES == 0
    cs = np.arange(n_chunk)[:, None] * CMP_STRIDE
    ss = np.arange(n_sel)[None, :] * SEL_BLOCK
    ovl = np.clip(np.minimum(cs + CMP_BLOCK, ss + SEL_BLOCK) - np.maximum(cs, ss), 0, None) / CMP_BLOCK
    ovl[n_cmp:] = 0.0
    ovl_t = jnp.asarray(ovl.T, dtype=F32)
    kern = functools.partial(_cmp_attn_kernel, tq=tq, n_sel=n_sel, n_top=n_top)
    return pl.pallas_call(
        kern,
        grid=(bsz, seq // tq),
        in_specs=[pl.BlockSpec((1, d_attn, tq), lambda b, i: (b, 0, i)),
                  pl.BlockSpec((1, N_KV_HEADS, n_chunk, HEAD_DIM), lambda b, i: (b, 0, 0, 0)),
                  pl.BlockSpec((1, N_KV_HEADS, HEAD_DIM, n_chunk), lambda b, i: (b, 0, 0, 0)),
                  pl.BlockSpec(pc.shape, lambda b, i: (0, 0, 0)),
                  pl.BlockSpec((n_sel, n_chunk), lambda b, i: (0, 0))],
        out_specs=[pl.BlockSpec((1, d_attn, tq), lambda b, i: (b, 0, i)),
                   pl.BlockSpec((1, N_KV_HEADS, n_sel, tq), lambda b, i: (b, 0, 0, i))],
        out_shape=[jax.ShapeDtypeStruct((bsz, d_attn, seq), F32),
                   jax.ShapeDtypeStruct((bsz, N_KV_HEADS, n_sel, seq), BF16)],
        scratch_shapes=[pltpu.VMEM((N_HEADS, CMP_NEAR_BACK + n_chunk, tq), F32)],
        compiler_params=_cparams("parallel", "parallel"),
        name="cmp_attn",
    )(qt, kc, vct, pc, ovl_t)


def _sw_attn_kernel(qt_ref, negm_ref, ks_ref, vst_ref, kw_ref, vwt_ref, ocmpt_ref, gt_ref,
                    pd_ref, p1_ref, wold_ref, eye_ref, o_ref, qa_ref, s_buf, acc_s, acc_w, *, tq, n_win_tiles):
    qi = pl.program_id(1)
    gates = _sigmoid(gt_ref[0])

    def tile(kt):
        return pl.ds(pl.multiple_of(kt * ATT_TILE, ATT_TILE), ATT_TILE)

    def pair(j):
        return pl.ds(pl.multiple_of(j * (2 * ATT_TILE), 2 * ATT_TILE), 2 * ATT_TILE)

    def col_max(s):
        return jnp.max(s, axis=0, keepdims=True)

    def exists(back):
        return jnp.where(qi >= back, 0.0, NEG)

    def near_bias(ref, hk):
        return jnp.concatenate([ref[hd] for hd in range(hk * GQA, (hk + 1) * GQA)], axis=1)

    def softmax_first(s_list, v_list, acc, hk):
        m = functools.reduce(jnp.maximum, [col_max(s) for s in s_list])
        acc[hk] = sum(_dot(v, jnp.exp(s - m).astype(BF16)) for v, s in zip(v_list, s_list))
        return m

    w_old = jnp.concatenate([wold_ref[...]] * GQA, axis=1)
    n_far = jnp.maximum(qi - 1, 0)
    n_pairs = n_far // 2
    k_prev = jnp.maximum(qi - 1, 0)
    k_odd = jnp.maximum(n_far - 1, 0)
    odd = jnp.where(n_far % 2 == 1, 0.0, NEG)

    m_sel = []
    c_max = []
    for hk in range(N_KV_HEADS):
        heads = range(hk * GQA, (hk + 1) * GQA)
        q4 = jnp.concatenate([qt_ref[0, hd * HEAD_DIM:(hd + 1) * HEAD_DIM, :] for hd in heads], axis=1)
        nm = negm_ref[0, hk]
        if nm.shape[0] < LANES - HEAD_DIM:
            nm = jnp.concatenate([nm, jnp.zeros((LANES - HEAD_DIM - nm.shape[0], tq), BF16)], axis=0)
        qa = jnp.concatenate([q4, jnp.concatenate([nm] * GQA, axis=1)], axis=0)
        qa_ref[hk] = qa
        qw = jnp.concatenate([q4, jnp.zeros_like(q4)], axis=0)
        pd4 = near_bias(pd_ref, hk)
        p14 = near_bias(p1_ref, hk)

        s_near = [_dot(ks_ref[0, hk, tile(qi), :], qa) + pd4,
                  _dot(ks_ref[0, hk, tile(k_prev), :], qa) + (p14 + exists(1)),
                  _dot(ks_ref[0, hk, tile(k_odd), :], qa) + odd]
        v_near = [vst_ref[0, hk, :, tile(qi)], vst_ref[0, hk, :, tile(k_prev)], vst_ref[0, hk, :, tile(k_odd)]]
        m_sel.append(softmax_first(s_near, v_near, acc_s, hk))

        s_win = [_dot(kw_ref[0, hk, tile(qi), :], qw) + pd4]
        v_win = [vwt_ref[0, hk, :, tile(qi)]]
        for back in range(1, n_win_tiles + 1):
            kt = jnp.maximum(qi - back, 0)
            s = _dot(kw_ref[0, hk, tile(kt), :], qw)
            if back == 1:
                s = s + (p14 + exists(back))
            elif back == n_win_tiles:
                s = s + (w_old + exists(back))
            else:
                s = s + exists(back)
            s_win.append(s)
            v_win.append(vwt_ref[0, hk, :, tile(kt)])
        softmax_first(s_win, v_win, acc_w, hk)

        s0 = _dot(ks_ref[0, hk, pair(0), :], qa)
        s_buf[hk] = s0
        c_max.append(col_max(s0))

    def far(j, carry):
        ms, cm = carry[:N_KV_HEADS], carry[N_KV_HEADS:]
        s_next = [_dot(ks_ref[0, hk, pair(j + 1), :], qa_ref[hk]) for hk in range(N_KV_HEADS)]
        m_new = []
        for hk in range(N_KV_HEADS):
            m = jnp.maximum(ms[hk], cm[hk])
            p = jnp.exp(s_buf[hk] - m).astype(BF16)
            acc_s[hk] = jnp.exp(ms[hk] - m) * acc_s[hk] + _dot(vst_ref[0, hk, :, pair(j)], p)
            m_new.append(m)
        c_new = []
        for hk in range(N_KV_HEADS):
            c_new.append(col_max(s_next[hk]))
            s_buf[hk] = s_next[hk]
        return (*m_new, *c_new)

    lax.fori_loop(0, n_pairs, far, (*m_sel, *c_max))

    ys = []
    for hk in range(N_KV_HEADS):
        o_sel = acc_s[hk, :HEAD_DIM] * (1.0 / acc_s[hk, HEAD_DIM:HEAD_DIM + 1])
        o_win = acc_w[hk, :HEAD_DIM] * (1.0 / acc_w[hk, HEAD_DIM:HEAD_DIM + 1])
        for g in range(GQA):
            hd = hk * GQA + g
            cols = slice(g * tq, (g + 1) * tq)
            ys.append(gates[3 * hd:3 * hd + 1] * ocmpt_ref[0, hd * HEAD_DIM:(hd + 1) * HEAD_DIM, :]
                      + gates[3 * hd + 1:3 * hd + 2] * o_sel[:, cols]
                      + gates[3 * hd + 2:3 * hd + 3] * o_win[:, cols])
    yt = jnp.concatenate(ys, axis=0).astype(BF16)
    o_ref[0] = _dot_nt(eye_ref[...], yt).astype(BF16)


def _sw_attn(qt, negm, ks, vst, kw, vwt, ocmpt, gt, pd, p1, w_old, tq):
    bsz, d_attn, seq = qt.shape
    n_sel = negm.shape[2]
    assert tq == ATT_TILE and WINDOW % ATT_TILE == 0 and n_sel <= LANES - HEAD_DIM and seq >= 2 * ATT_TILE
    eye = jnp.eye(tq, dtype=BF16)
    kern = functools.partial(_sw_attn_kernel, tq=tq, n_win_tiles=WINDOW // ATT_TILE)
    k_spec = pl.BlockSpec((1, N_KV_HEADS, seq, LANES), lambda b, i: (b, 0, 0, 0))
    vt_spec = pl.BlockSpec((1, N_KV_HEADS, V_ROWS, seq), lambda b, i: (b, 0, 0, 0))
    qt_spec = pl.BlockSpec((1, d_attn, tq), lambda b, i: (b, 0, i))
    full = lambda shape: pl.BlockSpec(shape, lambda b, i: (0,) * len(shape))
    return pl.pallas_call(
        kern,
        grid=(bsz, seq // tq),
        in_specs=[qt_spec,
                  pl.BlockSpec((1, N_KV_HEADS, n_sel, tq), lambda b, i: (b, 0, 0, i)),
                  k_spec, vt_spec, k_spec, vt_spec,
                  qt_spec,
                  pl.BlockSpec((1, gt.shape[1], tq), lambda b, i: (b, 0, i)),
                  full(pd.shape), full(p1.shape), full(w_old.shape), full(eye.shape)],
        out_specs=pl.BlockSpec((1, tq, d_attn), lambda b, i: (b, i, 0)),
        out_shape=jax.ShapeDtypeStruct((bsz, seq, d_attn), BF16),
        scratch_shapes=[pltpu.VMEM((N_KV_HEADS, LANES, GQA * tq), BF16),
                        pltpu.VMEM((N_KV_HEADS, 2 * ATT_TILE, GQA * tq), F32),
                        pltpu.VMEM((N_KV_HEADS, V_ROWS, GQA * tq), F32),
                        pltpu.VMEM((N_KV_HEADS, V_ROWS, GQA * tq), F32)],
        compiler_params=_cparams("parallel", "parallel"),
        name="sel_win_attn",
    )(qt, negm, ks, vst, kw, vwt, ocmpt, gt, pd, p1, w_old, eye)


def _outproj_kernel(x_ref, ys_ref, ya_ref, ws_ref, wa_ref, o_ref):
    o_ref[0] = x_ref[0] + _dot(ys_ref[...], ws_ref[...]) + _dot(ya_ref[0], wa_ref[...])


def _outproj(x, y_ssm_tm, y_attn, w_out, tm):
    bsz, seq, d = x.shape
    d_ssm = y_ssm_tm.shape[1] // bsz
    d_attn = y_attn.shape[2]
    wb = w_out.astype(BF16)
    ws, wa = wb[:d_ssm], wb[d_ssm:]
    full = lambda shape: pl.BlockSpec(shape, lambda b, i: (0,) * len(shape))
    return pl.pallas_call(
        _outproj_kernel,
        grid=(bsz, seq // tm),
        in_specs=[pl.BlockSpec((1, tm, d), lambda b, i: (b, i, 0)),
                  pl.BlockSpec((tm, d_ssm), lambda b, i: (i, b)),
                  pl.BlockSpec((1, tm, d_attn), lambda b, i: (b, i, 0)),
                  full(ws.shape), full(wa.shape)],
        out_specs=pl.BlockSpec((1, tm, d), lambda b, i: (b, i, 0)),
        out_shape=jax.ShapeDtypeStruct((bsz, seq, d), F32),
        compiler_params=_cparams("parallel", "parallel"),
        name="outproj",
    )(x, y_ssm_tm, y_attn, ws, wa)


def _router_kernel(x_ref, g_ref, wr_ref, br_ref, tri_ref, h_ref, route_ref, cnt_ref, run_ref):
    @pl.when(pl.program_id(0) == 0)
    def _():
        run_ref[...] = jnp.zeros_like(run_ref)

    x = x_ref[...]
    ms = jnp.mean(x * x, axis=-1, keepdims=True)
    h = x * lax.rsqrt(ms + EPS) * g_ref[...]
    h_ref[...] = h.astype(BF16)
    logits = _dot(h, wr_ref[...], HIGHEST) + br_ref[...]
    lane = lax.broadcasted_iota(jnp.int32, logits.shape, 1)
    is_grp = (lane >= N_EXPERTS) & (lane < N_EXPERTS + N_GROUPS)
    lg = jnp.where(is_grp, logits, NEG)
    mg = jnp.max(lg, axis=-1, keepdims=True)
    g_val = 1.0 / jnp.sum(jnp.where(is_grp, jnp.exp(lg - mg), 0.0), axis=-1, keepdims=True)
    g_idx = jnp.min(jnp.where(is_grp & (lg == mg), lane - N_EXPERTS, N_GROUPS), axis=-1, keepdims=True)
    in_grp = (lane >= g_idx * EXPERTS_PER_GROUP) & (lane < (g_idx + 1) * EXPERTS_PER_GROUP)
    le = jnp.where(in_grp, logits, NEG)
    me = jnp.max(le, axis=-1, keepdims=True)
    ee = jnp.where(in_grp, jnp.exp(le - me), 0.0)
    pe = ee / jnp.sum(ee, axis=-1, keepdims=True)
    p1 = jnp.max(pe, axis=-1, keepdims=True)
    i1 = jnp.min(jnp.where(in_grp & (pe == p1), lane, LANES), axis=-1, keepdims=True)
    rest = in_grp & (lane != i1)
    pr = jnp.where(rest, pe, -1.0)
    p2 = jnp.max(pr, axis=-1, keepdims=True)
    i2 = jnp.min(jnp.where(rest & (pr == p2), lane, LANES), axis=-1, keepdims=True)
    tot = p1 + p2
    oh1 = jnp.where(lane == i1, 1.0, 0.0)
    oh2 = jnp.where(lane == i2, 1.0, 0.0)
    both = oh1 + oh2
    before = _dot(tri_ref[...], both.astype(BF16)) + run_ref[...]
    r1 = jnp.sum(oh1 * before, axis=-1, keepdims=True)
    r2 = jnp.sum(oh2 * before, axis=-1, keepdims=True)
    run_ref[...] = run_ref[...] + jnp.sum(both, axis=0, keepdims=True)
    cnt_ref[...] = run_ref[...]
    vals = (i1.astype(F32), i2.astype(F32), g_val * (p1 / tot), g_val * (p2 / tot), r1, r2)
    out = jnp.zeros(logits.shape, F32)
    for k, val in enumerate(vals):
        out = jnp.where(lane == k, val, out)
    route_ref[...] = out


def _router(x2d, g, rg_w, rg_b, re_w, re_b, tm):
    tok, d = x2d.shape
    we = jnp.transpose(re_w, (1, 0, 2)).reshape(d, N_EXPERTS)
    wr = jnp.pad(jnp.concatenate([we, rg_w], axis=1), ((0, 0), (0, LANES - N_EXPERTS - N_GROUPS)))
    br = jnp.pad(jnp.concatenate([re_b.reshape(-1), rg_b]), (0, LANES - N_EXPERTS - N_GROUPS)).reshape(1, LANES)
    tri = jnp.asarray(np.tril(np.ones((tm, tm), np.float32), -1), dtype=BF16)
    full = lambda shape: pl.BlockSpec(shape, lambda i: (0,) * len(shape))
    return pl.pallas_call(
        _router_kernel,
        grid=(tok // tm,),
        in_specs=[pl.BlockSpec((tm, d), lambda i: (i, 0)), full((1, d)), full(wr.shape), full(br.shape),
                  full(tri.shape)],
        out_specs=[pl.BlockSpec((tm, d), lambda i: (i, 0)), pl.BlockSpec((tm, LANES), lambda i: (i, 0)),
                   full((1, LANES))],
        out_shape=[jax.ShapeDtypeStruct((tok, d), BF16), jax.ShapeDtypeStruct((tok, LANES), F32),
                   jax.ShapeDtypeStruct((1, LANES), F32)],
        scratch_shapes=[pltpu.VMEM((1, LANES), F32)],
        compiler_params=_cparams("arbitrary"),
        name="router",
    )(x2d, g.reshape(1, d), wr, br, tri)


def _dest_kernel(route_ref, start_ref, o_ref):
    r = route_ref[...]
    lane = lax.broadcasted_iota(jnp.int32, r.shape, 1)
    start = start_ref[...]
    d1 = r[:, 4:5] + jnp.sum(jnp.where(lane == r[:, 0:1].astype(jnp.int32), start, 0.0), axis=-1, keepdims=True)
    d2 = r[:, 5:6] + jnp.sum(jnp.where(lane == r[:, 1:2].astype(jnp.int32), start, 0.0), axis=-1, keepdims=True)
    o_ref[...] = jnp.where(lane == 0, d1, jnp.where(lane == 1, d2, 0.0)).astype(jnp.int32)


def _dest(route, cstart, tm):
    tok = route.shape[0]
    start = jnp.pad(cstart.astype(F32), (0, LANES - N_EXPERTS)).reshape(1, LANES)
    return pl.pallas_call(
        _dest_kernel,
        grid=(tok // tm,),
        in_specs=[pl.BlockSpec((tm, LANES), lambda i: (i, 0)), pl.BlockSpec((1, LANES), lambda i: (0, 0))],
        out_specs=pl.BlockSpec((tm, LANES), lambda i: (i, 0)),
        out_shape=jax.ShapeDtypeStruct((tok, LANES), jnp.int32),
        compiler_params=_cparams("parallel"),
        name="moe_dest",
    )(route, start)


MOE_ROW_TILE = 256


def _visit_plan(counts, n_pairs):
    cend = jnp.cumsum(counts)
    cstart = cend - counts
    first_tile = cstart // MOE_ROW_TILE
    last_tile = (cend - 1) // MOE_ROW_TILE
    n_vis = jnp.where(counts > 0, last_tile - first_tile + 1, 0)
    vis_end = jnp.cumsum(n_vis)
    n_visits = vis_end[-1]
    max_visits = n_pairs // MOE_ROW_TILE + N_EXPERTS - 1
    v = jnp.minimum(jnp.arange(max_visits, dtype=jnp.int32), n_visits - 1)
    ve = jnp.searchsorted(vis_end, v, side='right').astype(jnp.int32)
    vt = first_tile[ve] + (v - (vis_end[ve] - n_vis[ve]))
    vfirst = jnp.concatenate([jnp.ones((1,), jnp.int32), (vt[1:] != vt[:-1]).astype(jnp.int32)])
    meta = jnp.stack([ve, vt.astype(jnp.int32), vfirst, cstart[ve], cend[ve]]).astype(jnp.int32)
    return cstart, meta, n_visits.reshape(1).astype(jnp.int32)


def _moe_kernel(meta_ref, nv_ref, x_ref, w_ref, wg_ref, wu_ref, wd_ref, o_ref):
    v = pl.program_id(0)

    @pl.when(v < nv_ref[0])
    def _():
        row = meta_ref[1, v] * MOE_ROW_TILE + lax.broadcasted_iota(jnp.int32, (MOE_ROW_TILE, 1), 0)
        mine = (row >= meta_ref[3, v]) & (row < meta_ref[4, v])
        x = x_ref[...]
        a = _dot(x, wg_ref[0].astype(BF16))
        hid = jnp.where(mine, (a * _sigmoid(a)) * _dot(x, wu_ref[0].astype(BF16)) * w_ref[...], 0.0)
        y = _dot(hid.astype(BF16), wd_ref[0].astype(BF16))

        @pl.when(meta_ref[2, v] == 1)
        def _():
            o_ref[...] = y

        @pl.when(meta_ref[2, v] == 0)
        def _():
            o_ref[...] += y


def _moe_experts(x_sorted, w_sorted, meta, n_visits, w_gate, w_up, w_down):
    rows, d = x_sorted.shape
    f = w_gate.shape[-1]
    grid_spec = pltpu.PrefetchScalarGridSpec(
        num_scalar_prefetch=2,
        grid=(meta.shape[1],),
        in_specs=[pl.BlockSpec((MOE_ROW_TILE, d), lambda v, m, n: (m[1, v], 0)),
                  pl.BlockSpec((MOE_ROW_TILE, 1), lambda v, m, n: (m[1, v], 0)),
                  pl.BlockSpec((1, d, f), lambda v, m, n: (m[0, v], 0, 0)),
                  pl.BlockSpec((1, d, f), lambda v, m, n: (m[0, v], 0, 0)),
                  pl.BlockSpec((1, f, d), lambda v, m, n: (m[0, v], 0, 0))],
        out_specs=pl.BlockSpec((MOE_ROW_TILE, d), lambda v, m, n: (m[1, v], 0)),
    )
    return pl.pallas_call(
        _moe_kernel,
        grid_spec=grid_spec,
        out_shape=jax.ShapeDtypeStruct((rows, d), F32),
        compiler_params=_cparams("arbitrary"),
        name="moe_experts",
    )(meta, n_visits, x_sorted, w_sorted.reshape(rows, 1), w_gate.reshape(N_EXPERTS, d, f),
      w_up.reshape(N_EXPERTS, d, f), w_down.reshape(N_EXPERTS, f, d))


def _combine_kernel(x_ref, y1_ref, y2_ref, g_ref, o_ref, *, final_norm):
    x = x_ref[...] + y1_ref[...] + y2_ref[...]
    if final_norm:
        ms = jnp.mean(x * x, axis=-1, keepdims=True)
        x = x * lax.rsqrt(ms + EPS) * g_ref[...]
    o_ref[...] = x


def _combine(x2d, y1, y2, g, final_norm, tm):
    tok, d = x2d.shape
    blk = pl.BlockSpec((tm, d), lambda i: (i, 0))
    return pl.pallas_call(
        functools.partial(_combine_kernel, final_norm=final_norm),
        grid=(tok // tm,),
        in_specs=[blk, blk, blk, pl.BlockSpec((1, d), lambda i: (0, 0))],
        out_specs=blk,
        out_shape=jax.ShapeDtypeStruct((tok, d), F32),
        compiler_params=_cparams("parallel"),
        name="moe_combine",
    )(x2d, y1, y2, g.reshape(1, d))


def _moe(h, route, counts, x2d, w_gate, w_up, w_down, g_final, final_norm, tm):
    tok = x2d.shape[0]
    cstart, meta, n_visits = _visit_plan(counts[0, :N_EXPERTS].astype(jnp.int32), 2 * tok)
    dest = _dest(route, cstart, tm)
    d1, d2 = dest[:, 0], dest[:, 1]
    ptok = jnp.arange(tok, dtype=jnp.int32)
    _, tok_sorted, w_sorted = lax.sort((jnp.concatenate([d1, d2]), jnp.concatenate([ptok, ptok]),
                                        jnp.concatenate([route[:, 2], route[:, 3]])), num_keys=1)
    x_sorted = jnp.take(h, tok_sorted, axis=0)
    y_sorted = _moe_experts(x_sorted, w_sorted, meta, n_visits, w_gate, w_up, w_down)
    y1 = jnp.take(y_sorted, d1, axis=0)
    y2 = jnp.take(y_sorted, d2, axis=0)
    return _combine(x2d, y1, y2, g_final, final_norm, tm)


def kernel(x, norm1_g, norm2_g, final_g, w_in, w_out, ssm_a_re, ssm_a_im, ssm_b_re, ssm_b_im, ssm_c_re, ssm_c_im, ssm_d, ssm_log_dt, ssm_w_glu, cmp_pos, cmp_w1, cmp_w2, rel_bias, router_g_w, router_g_b, router_e_w, router_e_b, exp_w_gate, exp_w_up, exp_w_down):
    bsz, seq, d = x.shape
    depth = norm1_g.shape[0]
    tm = min(512, seq)
    tok_tile = min(1024, bsz * seq)
    pd, p1, w_old, pc = _bias_tables(rel_bias, seq)
    for l in range(depth):
        u_tm, k_cmp, v_cmp, ks, kw, qt, vst, vwt, gt = _inproj(x, norm1_g[l], w_in[l], tm)
        y_ssm = _s5(u_tm, bsz, ssm_a_re[l], ssm_a_im[l], ssm_b_re[l], ssm_b_im[l], ssm_c_re[l],
                    ssm_c_im[l], ssm_d[l].reshape(-1), ssm_log_dt[l], ssm_w_glu[l], tc=min(32, seq))
        kc, vct = _compress(k_cmp, v_cmp, cmp_pos[l], cmp_w1[l], cmp_w2[l])
        ocmpt, negm = _cmp_attn(qt, kc, vct, pc, ATT_TILE)
        y_attn = _sw_attn(qt, negm, ks, vst, kw, vwt, ocmpt, gt, pd, p1, w_old, ATT_TILE)
        x = _outproj(x, y_ssm, y_attn, w_out[l], tm)
        x2d = x.reshape(bsz * seq, d)
        h, route, counts = _router(x2d, norm2_g[l], router_g_w[l], router_g_b[l], router_e_w[l], router_e_b[l],
                                   tok_tile)
        x = _moe(h, route, counts, x2d, exp_w_gate[l], exp_w_up[l], exp_w_down[l], final_g, l == depth - 1,
                 tok_tile).reshape(bsz, seq, d)
    return x
```

```python
import functools
import math

import numpy as np
import jax
import jax.numpy as jnp
from jax import lax
from jax.experimental import pallas as pl
from jax.experimental.pallas import tpu as pltpu

F32 = jnp.float32
BF16 = jnp.bfloat16

SSM_GROUP = 16
SSM_STATE = 64
HEAD_DIM = 64
N_KV_HEADS = 2
GQA = 4
N_HEADS = N_KV_HEADS * GQA
CMP_BLOCK = 32
CMP_STRIDE = 16
SEL_BLOCK = 64
SEL_TOPK = 16
N_LOCAL_BLOCKS = 2
WINDOW = 512
N_BUCKETS = 32
MAX_DISTANCE = 128
N_GROUPS = 4
EXPERTS_PER_GROUP = 8
N_EXPERTS = N_GROUPS * EXPERTS_PER_GROUP
EPS = 1e-6
NEG = -1e30
BIG = 1e9

LANES = 128
SUBLANES = 8
ATT_TILE = 128
V_ROWS = HEAD_DIM + 16
SSM_OCT = 8
VMEM_LIMIT = 56 * 1024 * 1024

HIGHEST = lax.Precision.HIGHEST


def _cparams(*sem):
    return pltpu.CompilerParams(dimension_semantics=sem, vmem_limit_bytes=VMEM_LIMIT)


def _dot(a, b, precision=None):
    return jnp.dot(a, b, preferred_element_type=F32, precision=precision)


def _dot_nt(a, b, precision=None):
    return lax.dot_general(a, b, (((1,), (1,)), ((), ())),
                           preferred_element_type=F32, precision=precision)


def _gelu_tanh(x):
    return 0.5 * x * (1.0 + jnp.tanh(math.sqrt(2.0 / math.pi) * (x + 0.044715 * (x * x * x))))


def _sigmoid(x):
    return 1.0 / (1.0 + jnp.exp(-x))


def _t5_bucket_np(dist):
    n = np.maximum(dist, 0)
    max_exact = N_BUCKETS // 2
    nf = np.maximum(n, 1).astype(np.float32)
    large = max_exact + (np.log(nf / np.float32(max_exact))
                         / np.float32(math.log(MAX_DISTANCE / max_exact))
                         * np.float32(N_BUCKETS - max_exact)).astype(np.int32)
    large = np.minimum(large, N_BUCKETS - 1)
    return np.where(n < max_exact, n, large).astype(np.int32)


def _inproj_kernel(x_ref, g_ref, wu_ref, wc_ref, wk_ref, wqt_ref, wvt_ref, wgt_ref,
                   u_ref, kc_ref, vc_ref, ks_ref, kw_ref, qt_ref, vst_ref, vwt_ref, gt_ref, *, tm):
    x = x_ref[0]
    ms = jnp.mean(x * x, axis=-1, keepdims=True)
    h = (x * lax.rsqrt(ms + EPS) * g_ref[...]).astype(BF16)
    u_ref[...] = _dot(h, wu_ref[...])
    dkv = N_KV_HEADS * HEAD_DIM
    c = _dot(h, wc_ref[...])
    kc_ref[0] = c[:, :dkv]
    vc_ref[0] = c[:, dkv:]
    ka = _dot(h, wk_ref[...])
    blk = (pl.program_id(1) * tm + lax.broadcasted_iota(jnp.int32, (tm, LANES), 0)) // SEL_BLOCK
    onehot = jnp.where(lax.broadcasted_iota(jnp.int32, (tm, LANES), 1) - HEAD_DIM == blk, 1.0, 0.0)
    for hk in range(N_KV_HEADS):
        ks_ref[0, hk] = (ka[:, hk * LANES:(hk + 1) * LANES] + onehot).astype(BF16)
        kw_ref[0, hk] = ka[:, (N_KV_HEADS + hk) * LANES:(N_KV_HEADS + hk + 1) * LANES].astype(BF16)
    qt_ref[0] = (_dot_nt(wqt_ref[...], h) * (HEAD_DIM ** -0.5)).astype(BF16)
    vt = _dot_nt(wvt_ref[...], h).astype(BF16)
    ones_rows = jnp.where(lax.broadcasted_iota(jnp.int32, (V_ROWS - HEAD_DIM, tm), 0) == 0, 1.0, 0.0).astype(BF16)
    for hk in range(N_KV_HEADS):
        vst_ref[0, hk] = jnp.concatenate([vt[hk * HEAD_DIM:(hk + 1) * HEAD_DIM], ones_rows], axis=0)
        vwt_ref[0, hk] = jnp.concatenate([vt[dkv + hk * HEAD_DIM:dkv + (hk + 1) * HEAD_DIM], ones_rows], axis=0)
    gt_ref[0] = _dot_nt(wgt_ref[...], h)


def _inproj(x, g, w_in, tm):
    bsz, seq, d = x.shape
    d_ssm = d // 2
    d_attn = d - d_ssm
    dkv = N_KV_HEADS * HEAD_DIM
    o_kv = d_ssm + d_attn
    o_g = o_kv + 6 * dkv
    assert seq // SEL_BLOCK <= LANES - HEAD_DIM
    wb = w_in.astype(BF16)
    wu = wb[:, :d_ssm]
    wq = wb[:, d_ssm:o_kv]
    k_cmp, v_cmp, k_sel, v_sel, k_win, v_win = [wb[:, o_kv + i * dkv:o_kv + (i + 1) * dkv] for i in range(6)]
    wc = jnp.concatenate([k_cmp, v_cmp], axis=1)

    def pad_heads(w):
        w = w.reshape(d, N_KV_HEADS, HEAD_DIM)
        return jnp.pad(w, ((0, 0), (0, 0), (0, LANES - HEAD_DIM))).reshape(d, N_KV_HEADS * LANES)

    wk = jnp.concatenate([pad_heads(k_sel), pad_heads(k_win)], axis=1)
    wqt = wq.T
    wvt = jnp.concatenate([v_sel, v_win], axis=1).T
    n_gate_rows = 4 * SUBLANES
    wgt = jnp.pad(wb[:, o_g:], ((0, 0), (0, n_gate_rows - 3 * N_HEADS))).T
    full = lambda shape: pl.BlockSpec(shape, lambda b, i: (0,) * len(shape))
    k_spec = pl.BlockSpec((1, N_KV_HEADS, tm, LANES), lambda b, i: (b, 0, i, 0))
    vt_spec = pl.BlockSpec((1, N_KV_HEADS, V_ROWS, tm), lambda b, i: (b, 0, 0, i))
    k_shape = jax.ShapeDtypeStruct((bsz, N_KV_HEADS, seq, LANES), BF16)
    vt_shape = jax.ShapeDtypeStruct((bsz, N_KV_HEADS, V_ROWS, seq), BF16)
    return pl.pallas_call(
        functools.partial(_inproj_kernel, tm=tm),
        grid=(bsz, seq // tm),
        in_specs=[pl.BlockSpec((1, tm, d), lambda b, i: (b, i, 0)),
                  full((1, d)), full(wu.shape), full(wc.shape), full(wk.shape),
                  full(wqt.shape), full(wvt.shape), full(wgt.shape)],
        out_specs=[pl.BlockSpec((tm, d_ssm), lambda b, i: (i, b)),
                   pl.BlockSpec((1, tm, dkv), lambda b, i: (b, i, 0)),
                   pl.BlockSpec((1, tm, dkv), lambda b, i: (b, i, 0)),
                   k_spec, k_spec,
                   pl.BlockSpec((1, d_attn, tm), lambda b, i: (b, 0, i)),
                   vt_spec, vt_spec,
                   pl.BlockSpec((1, n_gate_rows, tm), lambda b, i: (b, 0, i))],
        out_shape=[jax.ShapeDtypeStruct((seq, bsz * d_ssm), F32),
                   jax.ShapeDtypeStruct((bsz, seq, dkv), F32),
                   jax.ShapeDtypeStruct((bsz, seq, dkv), F32),
                   k_shape, k_shape,
                   jax.ShapeDtypeStruct((bsz, d_attn, seq), BF16),
                   vt_shape, vt_shape,
                   jax.ShapeDtypeStruct((bsz, n_gate_rows, seq), F32)],
        compiler_params=_cparams("parallel", "parallel"),
        name="inproj",
    )(x, g.reshape(1, d), wu, wc, wk, wqt, wvt, wgt)


def _s5_kernel(u_ref, bmat_ref, cmat_ref, lam_ref, d_ref, wglu_ref, y_ref, bu_ref, st_ref,
               *, tc, bsz, n_oct, oct_c, oct_s, lane_chunk):
    rows = tc * bsz
    d_ssm = n_oct * oct_c
    n_state = n_oct * oct_s

    @pl.when(pl.program_id(0) == 0)
    def _():
        st_ref[...] = jnp.zeros_like(st_ref)

    u = u_ref[...].reshape(rows, d_ssm)
    ub = u.astype(BF16)
    for o in range(n_oct):
        r = _dot(ub[:, o * oct_c:(o + 1) * oct_c], bmat_ref[o])
        bu_ref[:, o * oct_s:(o + 1) * oct_s] = r[:, :oct_s]
        bu_ref[:, n_state + o * oct_s:n_state + (o + 1) * oct_s] = r[:, oct_s:]

    for c0 in range(0, n_state, lane_chunk):
        re = slice(c0, c0 + lane_chunk)
        im = slice(n_state + c0, n_state + c0 + lane_chunk)

        def step(t, carry, re=re, im=im):
            sr, si = carry
            row = pl.multiple_of(t * bsz, bsz)
            lr = lam_ref[:, re]
            li = lam_ref[:, im]
            nr = lr * sr - li * si + bu_ref[pl.ds(row, bsz), re]
            ni = lr * si + li * sr + bu_ref[pl.ds(row, bsz), im]
            bu_ref[pl.ds(row, bsz), re] = nr
            bu_ref[pl.ds(row, bsz), im] = ni
            return nr, ni

        sr, si = lax.fori_loop(0, tc, step, (st_ref[:, re], st_ref[:, im]))
        st_ref[:, re] = sr
        st_ref[:, im] = si

    ys = []
    for o in range(n_oct):
        s_re = bu_ref[:, o * oct_s:(o + 1) * oct_s].astype(BF16)
        s_im = bu_ref[:, n_state + o * oct_s:n_state + (o + 1) * oct_s].astype(BF16)
        ys.append(_dot(s_re, cmat_ref[o, :oct_s]) + _dot(s_im, cmat_ref[o, oct_s:]))
    y = jnp.concatenate(ys, axis=1) + d_ref[...] * u
    y = _gelu_tanh(y).astype(BF16)
    z = _dot(y, wglu_ref[...])
    out = z[:, :d_ssm] * _sigmoid(z[:, d_ssm:])
    y_ref[...] = out.astype(BF16).reshape(tc, bsz, d_ssm)


def _s5_params(a_re, a_im, b_re, b_im, c_re, c_im, log_dt, bsz):
    g, p = a_re.shape
    h = b_re.shape[-1]
    n_oct = g // SSM_OCT
    dt = jnp.exp(log_dt)[:, None]
    mag = jnp.exp(a_re * dt)
    lr = mag * jnp.cos(a_im * dt)
    li = mag * jnp.sin(a_im * dt)
    den = a_re * a_re + a_im * a_im
    kr = ((lr - 1.0) * a_re + li * a_im) / den
    ki = (li * a_re - (lr - 1.0) * a_im) / den
    bb_re = kr[..., None] * b_re - ki[..., None] * b_im
    bb_im = kr[..., None] * b_im + ki[..., None] * b_re
    eye = jnp.eye(SSM_OCT, dtype=F32)

    def bd_in(m):
        m = m.reshape(n_oct, SSM_OCT, p, h)
        return jnp.einsum('ogph,gk->oghkp', m, eye).reshape(n_oct, SSM_OCT * h, SSM_OCT * p)

    def bd_out(m):
        m = m.reshape(n_oct, SSM_OCT, h, p)
        return jnp.einsum('oghp,gk->ogpkh', m, eye).reshape(n_oct, SSM_OCT * p, SSM_OCT * h)

    bmat = jnp.concatenate([bd_in(bb_re), bd_in(bb_im)], axis=-1).astype(BF16)
    cmat = jnp.concatenate([bd_out(c_re), bd_out(-c_im)], axis=1).astype(BF16)
    lam = jnp.concatenate([lr.reshape(1, g * p), li.reshape(1, g * p)], axis=1)
    lam = jnp.broadcast_to(lam, (bsz, 2 * g * p))
    return bmat, cmat, lam


def _s5(u_tm, bsz, a_re, a_im, b_re, b_im, c_re, c_im, d, log_dt, w_glu, tc):
    seq = u_tm.shape[0]
    d_ssm = u_tm.shape[1] // bsz
    g, p = a_re.shape
    n_oct = g // SSM_OCT
    oct_c = SSM_OCT * SSM_GROUP
    oct_s = SSM_OCT * p
    bmat, cmat, lam = _s5_params(a_re, a_im, b_re, b_im, c_re, c_im, log_dt, bsz)
    u3 = u_tm.reshape(seq, bsz, d_ssm)
    full = lambda shape: pl.BlockSpec(shape, lambda i: (0,) * len(shape))
    kern = functools.partial(_s5_kernel, tc=tc, bsz=bsz, n_oct=n_oct, oct_c=oct_c, oct_s=oct_s,
                             lane_chunk=min(1024, g * p))
    y = pl.pallas_call(
        kern,
        grid=(seq // tc,),
        in_specs=[pl.BlockSpec((tc, bsz, d_ssm), lambda i: (i, 0, 0)),
                  full(bmat.shape), full(cmat.shape), full(lam.shape),
                  full((1, d_ssm)), full(w_glu.shape)],
        out_specs=pl.BlockSpec((tc, bsz, d_ssm), lambda i: (i, 0, 0)),
        out_shape=jax.ShapeDtypeStruct((seq, bsz, d_ssm), BF16),
        scratch_shapes=[pltpu.VMEM((tc * bsz, 2 * g * p), F32),
                        pltpu.VMEM((bsz, 2 * g * p), F32)],
        compiler_params=_cparams("arbitrary"),
        name="s5_mixer",
    )(u3, bmat, cmat, lam, d.reshape(1, d_ssm), w_glu.astype(BF16))
    return y.reshape(seq, bsz * d_ssm)


def _compress_kernel(k_ref, v_ref, w1a_ref, w1b_ref, posw_ref, w2_ref, w2t_ref, kc_ref, vct_ref):
    def hidden(i, src):
        x = src[0]
        a = _dot(x, w1a_ref[i], HIGHEST)
        b = _dot(x, w1b_ref[i], HIGHEST)
        return _gelu_tanh(a + pltpu.roll(b, a.shape[0] - 1, 0) + posw_ref[i])

    kc = _dot(hidden(0, k_ref), w2_ref[...], HIGHEST).astype(BF16)
    vct = _dot_nt(w2t_ref[...], hidden(1, v_ref), HIGHEST).astype(BF16)
    for hk in range(N_KV_HEADS):
        kc_ref[0, hk] = kc[:, hk * HEAD_DIM:(hk + 1) * HEAD_DIM]
        vct_ref[0, hk] = vct[hk * HEAD_DIM:(hk + 1) * HEAD_DIM]


def _compress(k_cmp, v_cmp, cmp_pos, cmp_w1, cmp_w2):
    bsz, seq, dkv = k_cmp.shape
    n_chunk = seq // CMP_STRIDE
    kview = k_cmp.reshape(bsz, n_chunk, CMP_STRIDE * dkv)
    vview = v_cmp.reshape(bsz, n_chunk, CMP_STRIDE * dkv)
    eye = jnp.eye(N_KV_HEADS, dtype=F32)
    w1 = jnp.einsum('ilde,hk->ilhdke', cmp_w1, eye)
    w1 = w1.reshape(2, CMP_BLOCK, dkv, dkv)
    w1a = w1[:, :CMP_STRIDE].reshape(2, CMP_STRIDE * dkv, dkv)
    w1b = w1[:, CMP_STRIDE:].reshape(2, CMP_STRIDE * dkv, dkv)
    posw = jnp.einsum('ild,ilde->ie', cmp_pos, cmp_w1, precision=HIGHEST)
    posw = jnp.tile(posw, (1, N_KV_HEADS)).reshape(2, 1, dkv)
    w2 = jnp.einsum('ief,hk->ihekf', cmp_w2, eye).reshape(2, dkv, dkv)
    full = lambda shape: pl.BlockSpec(shape, lambda b: (0,) * len(shape))
    return pl.pallas_call(
        _compress_kernel,
        grid=(bsz,),
        in_specs=[pl.BlockSpec((1, n_chunk, CMP_STRIDE * dkv), lambda b: (b, 0, 0)),
                  pl.BlockSpec((1, n_chunk, CMP_STRIDE * dkv), lambda b: (b, 0, 0)),
                  full(w1a.shape), full(w1b.shape), full(posw.shape), full((dkv, dkv)), full((dkv, dkv))],
        out_specs=[pl.BlockSpec((1, N_KV_HEADS, n_chunk, HEAD_DIM), lambda b: (b, 0, 0, 0)),
                   pl.BlockSpec((1, N_KV_HEADS, HEAD_DIM, n_chunk), lambda b: (b, 0, 0, 0))],
        out_shape=[jax.ShapeDtypeStruct((bsz, N_KV_HEADS, n_chunk, HEAD_DIM), BF16),
                   jax.ShapeDtypeStruct((bsz, N_KV_HEADS, HEAD_DIM, n_chunk), BF16)],
        compiler_params=_cparams("parallel"),
        name="compress",
    )(kview, vview, w1a, w1b, posw, w2[0], w2[1].T)


CMP_NEAR_BACK = 2 * SUBLANES
CMP_NEAR_ROWS = 3 * SUBLANES


def _bias_tables(rel_bias, seq):
    last = rel_bias[N_BUCKETS - 1]
    assert int(_t5_bucket_np(np.arange(ATT_TILE + 1, max(seq, 2 * ATT_TILE))).min()) == N_BUCKETS - 1
    s = np.arange(ATT_TILE)[:, None]
    t = np.arange(ATT_TILE)[None, :]

    def table(dist):
        return jnp.transpose(rel_bias[jnp.asarray(_t5_bucket_np(dist))] - last, (2, 0, 1))

    causal = jnp.asarray(t >= s)
    pd = jnp.where(causal, table(t - s), NEG)
    p1 = table(ATT_TILE + t - s)
    w_old = jnp.where(jnp.asarray(s > t), 0.0, NEG).astype(F32)
    m = np.arange(CMP_NEAR_ROWS)[:, None] - CMP_NEAR_BACK
    dist_c = t - (CMP_STRIDE * m + CMP_BLOCK - 1)
    assert dist_c[0].min() > ATT_TILE
    pc = jnp.where(jnp.asarray(dist_c >= 0), table(dist_c), 0.0)
    return pd, p1, w_old, pc


def _cmp_attn_kernel(qt_ref, kc_ref, vct_ref, pc_ref, ovl_ref, ocmpt_ref, negm_ref, s_scr,
                     *, tq, n_sel, n_top):
    qi = pl.program_id(1)
    n_chunk = kc_ref.shape[2]
    n_i = lax.broadcasted_iota(jnp.int32, (n_chunk, tq), 0)
    t_i = qi * tq + lax.broadcasted_iota(jnp.int32, (n_chunk, tq), 1)
    valid = n_i * CMP_STRIDE + (CMP_BLOCK - 1) <= t_i
    jrow = lax.broadcasted_iota(jnp.int32, (n_sel, tq), 0)
    blk_t = (qi * tq + lax.broadcasted_iota(jnp.int32, (n_sel, tq), 1)) // SEL_BLOCK
    forced = (jrow == 0) | ((jrow <= blk_t) & (jrow > blk_t - N_LOCAL_BLOCKS))
    future = jrow > blk_t
    jl = lax.broadcasted_iota(jnp.int32, (SUBLANES, tq), 0)
    for hd in range(N_HEADS):
        s_scr[hd, 0:CMP_NEAR_BACK, :] = jnp.zeros((CMP_NEAR_BACK, tq), F32)
    near = pl.ds(pl.multiple_of(qi * (tq // CMP_STRIDE), SUBLANES), CMP_NEAR_ROWS)
    body = slice(CMP_NEAR_BACK, CMP_NEAR_BACK + n_chunk)
    for hk in range(N_KV_HEADS):
        kc = kc_ref[0, hk]
        vct = vct_ref[0, hk]
        psum = jnp.zeros((n_chunk, tq), F32)
        for g in range(GQA):
            hd = hk * GQA + g
            rows = slice(hd * HEAD_DIM, (hd + 1) * HEAD_DIM)
            s_scr[hd, body, :] = _dot(kc, qt_ref[0, rows, :])
            s_scr[hd, near, :] = s_scr[hd, near, :] + pc_ref[hd]
            s = jnp.where(valid, s_scr[hd, body, :], NEG)
            m = jnp.max(s, axis=0, keepdims=True)
            e = jnp.where(valid, jnp.exp(s - m), 0.0)
            l = jnp.sum(e, axis=0, keepdims=True)
            p = e * jnp.where(l > 0.0, 1.0 / l, 0.0)
            psum = psum + p
            ocmpt_ref[0, rows, :] = _dot(vct, p.astype(BF16))
        imp = _dot(ovl_ref[...], psum, HIGHEST)
        v = jnp.where(forced, BIG, jnp.where(future, -BIG, imp))
        n_t = n_sel // SUBLANES
        vt = [v[a * SUBLANES:(a + 1) * SUBLANES] for a in range(n_t)]
        rank = [jnp.zeros((SUBLANES, tq), F32) for _ in range(n_t)]
        for jp in range(n_sel):
            row = jnp.broadcast_to(v[jp:jp + 1, :], (SUBLANES, tq))
            for a in range(n_t):
                if a > jp // SUBLANES:
                    hit = jnp.where(row >= vt[a], 1.0, 0.0)
                elif a < jp // SUBLANES:
                    hit = jnp.where(row > vt[a], 1.0, 0.0)
                else:
                    hit = jnp.where(jl > jp % SUBLANES, jnp.where(row >= vt[a], 1.0, 0.0),
                                    jnp.where(row > vt[a], 1.0, 0.0))
                rank[a] = rank[a] + hit
        rank = jnp.concatenate(rank, axis=0)
        negm_ref[0, hk] = jnp.where(rank < float(n_top), 0.0, NEG).astype(BF16)


def _cmp_attn(qt, kc, vct, pc, tq):
    bsz, d_attn, seq = qt.shape
    n_chunk = kc.shape[2]
    n_sel = seq // SEL_BLOCK
    n_top = min(SEL_TOPK, n_sel)
    n_cmp = (seq - CMP_BLOCK) // CMP_STRIDE + 1
    assert n_sel % SUBLANES == 0 and tq % CMP_STRIDE == 0 and (tq // CMP_STRIDE) % SUBLANES == 0
    cs = np.arange(n_chunk)[:, None] * CMP_STRIDE
    ss = np.arange(n_sel)[None, :] * SEL_BLOCK
    ovl = np.clip(np.minimum(cs + CMP_BLOCK, ss + SEL_BLOCK) - np.maximum(cs, ss), 0, None) / CMP_BLOCK
    ovl[n_cmp:] = 0.0
    ovl_t = jnp.asarray(ovl.T, dtype=F32)
    kern = functools.partial(_cmp_attn_kernel, tq=tq, n_sel=n_sel, n_top=n_top)
    return pl.pallas_call(
        kern,
        grid=(bsz, seq // tq),
        in_specs=[pl.BlockSpec((1, d_attn, tq), lambda b, i: (b, 0, i)),
                  pl.BlockSpec((1, N_KV_HEADS, n_chunk, HEAD_DIM), lambda b, i: (b, 0, 0, 0)),
                  pl.BlockSpec((1, N_KV_HEADS, HEAD_DIM, n_chunk), lambda b, i: (b, 0, 0, 0)),
                  pl.BlockSpec(pc.shape, lambda b, i: (0, 0, 0)),
                  pl.BlockSpec((n_sel, n_chunk), lambda b, i: (0, 0))],
        out_specs=[pl.BlockSpec((1, d_attn, tq), lambda b, i: (b, 0, i)),
                   pl.BlockSpec((1, N_KV_HEADS, n_sel, tq), lambda b, i: (b, 0, 0, i))],
        out_shape=[jax.ShapeDtypeStruct((bsz, d_attn, seq), F32),
                   jax.ShapeDtypeStruct((bsz, N_KV_HEADS, n_sel, seq), BF16)],
        scratch_shapes=[pltpu.VMEM((N_HEADS, CMP_NEAR_BACK + n_chunk, tq), F32)],
        compiler_params=_cparams("parallel", "parallel"),
        name="cmp_attn",
    )(qt, kc, vct, pc, ovl_t)


def _sw_attn_kernel(qt_ref, negm_ref, ks_ref, vst_ref, kw_ref, vwt_ref, ocmpt_ref, gt_ref,
                    pd_ref, p1_ref, wold_ref, eye_ref, o_ref, qa_ref, s_buf, acc_s, acc_w, *, tq, n_win_tiles):
    qi = pl.program_id(1)
    gates = _sigmoid(gt_ref[0])

    def tile(kt):
        return pl.ds(pl.multiple_of(kt * ATT_TILE, ATT_TILE), ATT_TILE)

    def pair(j):
        return pl.ds(pl.multiple_of(j * (2 * ATT_TILE), 2 * ATT_TILE), 2 * ATT_TILE)

    def col_max(s):
        return jnp.max(s, axis=0, keepdims=True)

    def exists(back):
        return jnp.where(qi >= back, 0.0, NEG)

    def near_bias(ref, hk):
        return jnp.concatenate([ref[hd] for hd in range(hk * GQA, (hk + 1) * GQA)], axis=1)

    def softmax_first(s_list, v_list, acc, hk):
        m = functools.reduce(jnp.maximum, [col_max(s) for s in s_list])
        acc[hk] = sum(_dot(v, jnp.exp(s - m).astype(BF16)) for v, s in zip(v_list, s_list))
        return m

    w_old = jnp.concatenate([wold_ref[...]] * GQA, axis=1)
    n_far = jnp.maximum(qi - 1, 0)
    n_pairs = n_far // 2
    k_prev = jnp.maximum(qi - 1, 0)
    k_odd = jnp.maximum(n_far - 1, 0)
    odd = jnp.where(n_far % 2 == 1, 0.0, NEG)

    m_sel = []
    c_max = []
    for hk in range(N_KV_HEADS):
        heads = range(hk * GQA, (hk + 1) * GQA)
        q4 = jnp.concatenate([qt_ref[0, hd * HEAD_DIM:(hd + 1) * HEAD_DIM, :] for hd in heads], axis=1)
        nm = negm_ref[0, hk]
        if nm.shape[0] < LANES - HEAD_DIM:
            nm = jnp.concatenate([nm, jnp.zeros((LANES - HEAD_DIM - nm.shape[0], tq), BF16)], axis=0)
        qa = jnp.concatenate([q4, jnp.concatenate([nm] * GQA, axis=1)], axis=0)
        qa_ref[hk] = qa
        qw = jnp.concatenate([q4, jnp.zeros_like(q4)], axis=0)
        pd4 = near_bias(pd_ref, hk)
        p14 = near_bias(p1_ref, hk)

        s_near = [_dot(ks_ref[0, hk, tile(qi), :], qa) + pd4,
                  _dot(ks_ref[0, hk, tile(k_prev), :], qa) + (p14 + exists(1)),
                  _dot(ks_ref[0, hk, tile(k_odd), :], qa) + odd]
        v_near = [vst_ref[0, hk, :, tile(qi)], vst_ref[0, hk, :, tile(k_prev)], vst_ref[0, hk, :, tile(k_odd)]]
        m_sel.append(softmax_first(s_near, v_near, acc_s, hk))

        s_win = [_dot(kw_ref[0, hk, tile(qi), :], qw) + pd4]
        v_win = [vwt_ref[0, hk, :, tile(qi)]]
        for back in range(1, n_win_tiles + 1):
            kt = jnp.maximum(qi - back, 0)
            s = _dot(kw_ref[0, hk, tile(kt), :], qw)
            if back == 1:
                s = s + (p14 + exists(back))
            elif back == n_win_tiles:
                s = s + (w_old + exists(back))
            else:
                s = s + exists(back)
            s_win.append(s)
            v_win.append(vwt_ref[0, hk, :, tile(kt)])
        softmax_first(s_win, v_win, acc_w, hk)

        s0 = _dot(ks_ref[0, hk, pair(0), :], qa)
        s_buf[hk] = s0
        c_max.append(col_max(s0))

    def far(j, carry):
        ms, cm = carry[:N_KV_HEADS], carry[N_KV_HEADS:]
        s_next = [_dot(ks_ref[0, hk, pair(j + 1), :], qa_ref[hk]) for hk in range(N_KV_HEADS)]
        m_new = []
        for hk in range(N_KV_HEADS):
            m = jnp.maximum(ms[hk], cm[hk])
            p = jnp.exp(s_buf[hk] - m).astype(BF16)
            acc_s[hk] = jnp.exp(ms[hk] - m) * acc_s[hk] + _dot(vst_ref[0, hk, :, pair(j)], p)
            m_new.append(m)
        c_new = []
        for hk in range(N_KV_HEADS):
            c_new.append(col_max(s_next[hk]))
            s_buf[hk] = s_next[hk]
        return (*m_new, *c_new)

    lax.fori_loop(0, n_pairs, far, (*m_sel, *c_max))

    ys = []
    for hk in range(N_KV_HEADS):
        o_sel = acc_s[hk, :HEAD_DIM] * (1.0 / acc_s[hk, HEAD_DIM:HEAD_DIM + 1])
        o_win = acc_w[hk, :HEAD_DIM] * (1.0 / acc_w[hk, HEAD_DIM:HEAD_DIM + 1])
        for g in range(GQA):
            hd = hk * GQA + g
            cols = slice(g * tq, (g + 1) * tq)
            ys.append(gates[3 * hd:3 * hd + 1] * ocmpt_ref[0, hd * HEAD_DIM:(hd + 1) * HEAD_DIM, :]
                      + gates[3 * hd + 1:3 * hd + 2] * o_sel[:, cols]
                      + gates[3 * hd + 2:3 * hd + 3] * o_win[:, cols])
    yt = jnp.concatenate(ys, axis=0).astype(BF16)
    o_ref[0] = _dot_nt(eye_ref[...], yt).astype(BF16)


def _sw_attn(qt, negm, ks, vst, kw, vwt, ocmpt, gt, pd, p1, w_old, tq):
    bsz, d_attn, seq = qt.shape
    n_sel = negm.shape[2]
    assert tq == ATT_TILE and WINDOW % ATT_TILE == 0 and n_sel <= LANES - HEAD_DIM and seq >= 2 * ATT_TILE
    eye = jnp.eye(tq, dtype=BF16)
    kern = functools.partial(_sw_attn_kernel, tq=tq, n_win_tiles=WINDOW // ATT_TILE)
    k_spec = pl.BlockSpec((1, N_KV_HEADS, seq, LANES), lambda b, i: (b, 0, 0, 0))
    vt_spec = pl.BlockSpec((1, N_KV_HEADS, V_ROWS, seq), lambda b, i: (b, 0, 0, 0))
    qt_spec = pl.BlockSpec((1, d_attn, tq), lambda b, i: (b, 0, i))
    full = lambda shape: pl.BlockSpec(shape, lambda b, i: (0,) * len(shape))
    return pl.pallas_call(
        kern,
        grid=(bsz, seq // tq),
        in_specs=[qt_spec,
                  pl.BlockSpec((1, N_KV_HEADS, n_sel, tq), lambda b, i: (b, 0, 0, i)),
                  k_spec, vt_spec, k_spec, vt_spec,
                  qt_spec,
                  pl.BlockSpec((1, gt.shape[1], tq), lambda b, i: (b, 0, i)),
                  full(pd.shape), full(p1.shape), full(w_old.shape), full(eye.shape)],
        out_specs=pl.BlockSpec((1, tq, d_attn), lambda b, i: (b, i, 0)),
        out_shape=jax.ShapeDtypeStruct((bsz, seq, d_attn), BF16),
        scratch_shapes=[pltpu.VMEM((N_KV_HEADS, LANES, GQA * tq), BF16),
                        pltpu.VMEM((N_KV_HEADS, 2 * ATT_TILE, GQA * tq), F32),
                        pltpu.VMEM((N_KV_HEADS, V_ROWS, GQA * tq), F32),
                        pltpu.VMEM((N_KV_HEADS, V_ROWS, GQA * tq), F32)],
        compiler_params=_cparams("parallel", "parallel"),
        name="sel_win_attn",
    )(qt, negm, ks, vst, kw, vwt, ocmpt, gt, pd, p1, w_old, eye)


def _outproj_kernel(x_ref, ys_ref, ya_ref, ws_ref, wa_ref, o_ref):
    o_ref[0] = x_ref[0] + _dot(ys_ref[...], ws_ref[...]) + _dot(ya_ref[0], wa_ref[...])


def _outproj(x, y_ssm_tm, y_attn, w_out, tm):
    bsz, seq, d = x.shape
    d_ssm = y_ssm_tm.shape[1] // bsz
    d_attn = y_attn.shape[2]
    wb = w_out.astype(BF16)
    ws, wa = wb[:d_ssm], wb[d_ssm:]
    full = lambda shape: pl.BlockSpec(shape, lambda b, i: (0,) * len(shape))
    return pl.pallas_call(
        _outproj_kernel,
        grid=(bsz, seq // tm),
        in_specs=[pl.BlockSpec((1, tm, d), lambda b, i: (b, i, 0)),
                  pl.BlockSpec((tm, d_ssm), lambda b, i: (i, b)),
                  pl.BlockSpec((1, tm, d_attn), lambda b, i: (b, i, 0)),
                  full(ws.shape), full(wa.shape)],
        out_specs=pl.BlockSpec((1, tm, d), lambda b, i: (b, i, 0)),
        out_shape=jax.ShapeDtypeStruct((bsz, seq, d), F32),
        compiler_params=_cparams("parallel", "parallel"),
        name="outproj",
    )(x, y_ssm_tm, y_attn, ws, wa)


def _router_kernel(x_ref, g_ref, wr_ref, br_ref, tri_ref, h_ref, route_ref, cnt_ref, run_ref):
    @pl.when(pl.program_id(0) == 0)
    def _():
        run_ref[...] = jnp.zeros_like(run_ref)

    x = x_ref[...]
    ms = jnp.mean(x * x, axis=-1, keepdims=True)
    h = x * lax.rsqrt(ms + EPS) * g_ref[...]
    h_ref[...] = h.astype(BF16)
    logits = _dot(h, wr_ref[...], HIGHEST) + br_ref[...]
    lane = lax.broadcasted_iota(jnp.int32, logits.shape, 1)
    is_grp = (lane >= N_EXPERTS) & (lane < N_EXPERTS + N_GROUPS)
    lg = jnp.where(is_grp, logits, NEG)
    mg = jnp.max(lg, axis=-1, keepdims=True)
    g_val = 1.0 / jnp.sum(jnp.where(is_grp, jnp.exp(lg - mg), 0.0), axis=-1, keepdims=True)
    g_idx = jnp.min(jnp.where(is_grp & (lg == mg), lane - N_EXPERTS, N_GROUPS), axis=-1, keepdims=True)
    in_grp = (lane >= g_idx * EXPERTS_PER_GROUP) & (lane < (g_idx + 1) * EXPERTS_PER_GROUP)
    le = jnp.where(in_grp, logits, NEG)
    me = jnp.max(le, axis=-1, keepdims=True)
    ee = jnp.where(in_grp, jnp.exp(le - me), 0.0)
    pe = ee / jnp.sum(ee, axis=-1, keepdims=True)
    p1 = jnp.max(pe, axis=-1, keepdims=True)
    i1 = jnp.min(jnp.where(in_grp & (pe == p1), lane, LANES), axis=-1, keepdims=True)
    rest = in_grp & (lane != i1)
    pr = jnp.where(rest, pe, -1.0)
    p2 = jnp.max(pr, axis=-1, keepdims=True)
    i2 = jnp.min(jnp.where(rest & (pr == p2), lane, LANES), axis=-1, keepdims=True)
    tot = p1 + p2
    oh1 = jnp.where(lane == i1, 1.0, 0.0)
    oh2 = jnp.where(lane == i2, 1.0, 0.0)
    both = oh1 + oh2
    before = _dot(tri_ref[...], both.astype(BF16)) + run_ref[...]
    r1 = jnp.sum(oh1 * before, axis=-1, keepdims=True)
    r2 = jnp.sum(oh2 * before, axis=-1, keepdims=True)
    run_ref[...] = run_ref[...] + jnp.sum(both, axis=0, keepdims=True)
    cnt_ref[...] = run_ref[...]
    vals = (i1.astype(F32), i2.astype(F32), g_val * (p1 / tot), g_val * (p2 / tot), r1, r2)
    out = jnp.zeros(logits.shape, F32)
    for k, val in enumerate(vals):
        out = jnp.where(lane == k, val, out)
    route_ref[...] = out


def _router(x2d, g, rg_w, rg_b, re_w, re_b, tm):
    tok, d = x2d.shape
    we = jnp.transpose(re_w, (1, 0, 2)).reshape(d, N_EXPERTS)
    wr = jnp.pad(jnp.concatenate([we, rg_w], axis=1), ((0, 0), (0, LANES - N_EXPERTS - N_GROUPS)))
    br = jnp.pad(jnp.concatenate([re_b.reshape(-1), rg_b]), (0, LANES - N_EXPERTS - N_GROUPS)).reshape(1, LANES)
    tri = jnp.asarray(np.tril(np.ones((tm, tm), np.float32), -1), dtype=BF16)
    full = lambda shape: pl.BlockSpec(shape, lambda i: (0,) * len(shape))
    return pl.pallas_call(
        _router_kernel,
        grid=(tok // tm,),
        in_specs=[pl.BlockSpec((tm, d), lambda i: (i, 0)), full((1, d)), full(wr.shape), full(br.shape),
                  full(tri.shape)],
        out_specs=[pl.BlockSpec((tm, d), lambda i: (i, 0)), pl.BlockSpec((tm, LANES), lambda i: (i, 0)),
                   full((1, LANES))],
        out_shape=[jax.ShapeDtypeStruct((tok, d), BF16), jax.ShapeDtypeStruct((tok, LANES), F32),
                   jax.ShapeDtypeStruct((1, LANES), F32)],
        scratch_shapes=[pltpu.VMEM((1, LANES), F32)],
        compiler_params=_cparams("arbitrary"),
        name="router",
    )(x2d, g.reshape(1, d), wr, br, tri)


def _dest_kernel(route_ref, start_ref, o_ref):
    r = route_ref[...]
    lane = lax.broadcasted_iota(jnp.int32, r.shape, 1)
    start = start_ref[...]
    d1 = r[:, 4:5] + jnp.sum(jnp.where(lane == r[:, 0:1].astype(jnp.int32), start, 0.0), axis=-1, keepdims=True)
    d2 = r[:, 5:6] + jnp.sum(jnp.where(lane == r[:, 1:2].astype(jnp.int32), start, 0.0), axis=-1, keepdims=True)
    o_ref[...] = jnp.where(lane == 0, d1, jnp.where(lane == 1, d2, 0.0)).astype(jnp.int32)


def _dest(route, cstart, tm):
    tok = route.shape[0]
    start = jnp.pad(cstart.astype(F32), (0, LANES - N_EXPERTS)).reshape(1, LANES)
    return pl.pallas_call(
        _dest_kernel,
        grid=(tok // tm,),
        in_specs=[pl.BlockSpec((tm, LANES), lambda i: (i, 0)), pl.BlockSpec((1, LANES), lambda i: (0, 0))],
        out_specs=pl.BlockSpec((tm, LANES), lambda i: (i, 0)),
        out_shape=jax.ShapeDtypeStruct((tok, LANES), jnp.int32),
        compiler_params=_cparams("parallel"),
        name="moe_dest",
    )(route, start)


MOE_ROW_TILE = 256


def _visit_plan(counts, n_pairs):
    cend = jnp.cumsum(counts)
    cstart = cend - counts
    first_tile = cstart // MOE_ROW_TILE
    last_tile = (cend - 1) // MOE_ROW_TILE
    n_vis = jnp.where(counts > 0, last_tile - first_tile + 1, 0)
    vis_end = jnp.cumsum(n_vis)
    n_visits = vis_end[-1]
    max_visits = n_pairs // MOE_ROW_TILE + N_EXPERTS - 1
    v = jnp.minimum(jnp.arange(max_visits, dtype=jnp.int32), n_visits - 1)
    ve = jnp.searchsorted(vis_end, v, side='right').astype(jnp.int32)
    vt = first_tile[ve] + (v - (vis_end[ve] - n_vis[ve]))
    vfirst = jnp.concatenate([jnp.ones((1,), jnp.int32), (vt[1:] != vt[:-1]).astype(jnp.int32)])
    meta = jnp.stack([ve, vt.astype(jnp.int32), vfirst, cstart[ve], cend[ve]]).astype(jnp.int32)
    return cstart, meta, n_visits.reshape(1).astype(jnp.int32)


def _moe_kernel(meta_ref, nv_ref, x_ref, w_ref, wg_ref, wu_ref, wd_ref, o_ref):
    v = pl.program_id(0)

    @pl.when(v < nv_ref[0])
    def _():
        row = meta_ref[1, v] * MOE_ROW_TILE + lax.broadcasted_iota(jnp.int32, (MOE_ROW_TILE, 1), 0)
        mine = (row >= meta_ref[3, v]) & (row < meta_ref[4, v])
        x = x_ref[...]
        a = _dot(x, wg_ref[0].astype(BF16))
        hid = jnp.where(mine, (a * _sigmoid(a)) * _dot(x, wu_ref[0].astype(BF16)) * w_ref[...], 0.0)
        y = _dot(hid.astype(BF16), wd_ref[0].astype(BF16))

        @pl.when(meta_ref[2, v] == 1)
        def _():
            o_ref[...] = y

        @pl.when(meta_ref[2, v] == 0)
        def _():
            o_ref[...] += y


def _moe_experts(x_sorted, w_sorted, meta, n_visits, w_gate, w_up, w_down):
    rows, d = x_sorted.shape
    f = w_gate.shape[-1]
    grid_spec = pltpu.PrefetchScalarGridSpec(
        num_scalar_prefetch=2,
        grid=(meta.shape[1],),
        in_specs=[pl.BlockSpec((MOE_ROW_TILE, d), lambda v, m, n: (m[1, v], 0)),
                  pl.BlockSpec((MOE_ROW_TILE, 1), lambda v, m, n: (m[1, v], 0)),
                  pl.BlockSpec((1, d, f), lambda v, m, n: (m[0, v], 0, 0)),
                  pl.BlockSpec((1, d, f), lambda v, m, n: (m[0, v], 0, 0)),
                  pl.BlockSpec((1, f, d), lambda v, m, n: (m[0, v], 0, 0))],
        out_specs=pl.BlockSpec((MOE_ROW_TILE, d), lambda v, m, n: (m[1, v], 0)),
    )
    return pl.pallas_call(
        _moe_kernel,
        grid_spec=grid_spec,
        out_shape=jax.ShapeDtypeStruct((rows, d), F32),
        compiler_params=_cparams("arbitrary"),
        name="moe_experts",
    )(meta, n_visits, x_sorted, w_sorted.reshape(rows, 1), w_gate.reshape(N_EXPERTS, d, f),
      w_up.reshape(N_EXPERTS, d, f), w_down.reshape(N_EXPERTS, f, d))


def _combine_kernel(x_ref, y1_ref, y2_ref, g_ref, o_ref, *, final_norm):
    x = x_ref[...] + y1_ref[...] + y2_ref[...]
    if final_norm:
        ms = jnp.mean(x * x, axis=-1, keepdims=True)
        x = x * lax.rsqrt(ms + EPS) * g_ref[...]
    o_ref[...] = x


def _combine(x2d, y1, y2, g, final_norm, tm):
    tok, d = x2d.shape
    blk = pl.BlockSpec((tm, d), lambda i: (i, 0))
    return pl.pallas_call(
        functools.partial(_combine_kernel, final_norm=final_norm),
        grid=(tok // tm,),
        in_specs=[blk, blk, blk, pl.BlockSpec((1, d), lambda i: (0, 0))],
        out_specs=blk,
        out_shape=jax.ShapeDtypeStruct((tok, d), F32),
        compiler_params=_cparams("parallel"),
        name="moe_combine",
    )(x2d, y1, y2, g.reshape(1, d))


def _moe(h, route, counts, x2d, w_gate, w_up, w_down, g_final, final_norm, tm):
    tok = x2d.shape[0]
    cstart, meta, n_visits = _visit_plan(counts[0, :N_EXPERTS].astype(jnp.int32), 2 * tok)
    dest = _dest(route, cstart, tm)
    d1, d2 = dest[:, 0], dest[:, 1]
    ptok = jnp.arange(tok, dtype=jnp.int32)
    _, tok_sorted, w_sorted = lax.sort((jnp.concatenate([d1, d2]), jnp.concatenate([ptok, ptok]),
                                        jnp.concatenate([route[:, 2], route[:, 3]])), num_keys=1)
    x_sorted = jnp.take(h, tok_sorted, axis=0, mode="clip")
    y_sorted = _moe_experts(x_sorted, w_sorted, meta, n_visits, w_gate, w_up, w_down)
    y1 = jnp.take(y_sorted, d1, axis=0, mode="clip")
    y2 = jnp.take(y_sorted, d2, axis=0, mode="clip")
    return _combine(x2d, y1, y2, g_final, final_norm, tm)


def kernel(x, norm1_g, norm2_g, final_g, w_in, w_out, ssm_a_re, ssm_a_im, ssm_b_re, ssm_b_im, ssm_c_re, ssm_c_im, ssm_d, ssm_log_dt, ssm_w_glu, cmp_pos, cmp_w1, cmp_w2, rel_bias, router_g_w, router_g_b, router_e_w, router_e_b, exp_w_gate, exp_w_up, exp_w_down):
    bsz, seq, d = x.shape
    depth = norm1_g.shape[0]
    tm = min(512, seq)
    tok_tile = min(1024, bsz * seq)
    pd, p1, w_old, pc = _bias_tables(rel_bias, seq)
    for l in range(depth):
        u_tm, k_cmp, v_cmp, ks, kw, qt, vst, vwt, gt = _inproj(x, norm1_g[l], w_in[l], tm)
        y_ssm = _s5(u_tm, bsz, ssm_a_re[l], ssm_a_im[l], ssm_b_re[l], ssm_b_im[l], ssm_c_re[l],
                    ssm_c_im[l], ssm_d[l].reshape(-1), ssm_log_dt[l], ssm_w_glu[l], tc=min(32, seq))
        kc, vct = _compress(k_cmp, v_cmp, cmp_pos[l], cmp_w1[l], cmp_w2[l])
        ocmpt, negm = _cmp_attn(qt, kc, vct, pc, ATT_TILE)
        y_attn = _sw_attn(qt, negm, ks, vst, kw, vwt, ocmpt, gt, pd, p1, w_old, ATT_TILE)
        x = _outproj(x, y_ssm, y_attn, w_out[l], tm)
        x2d = x.reshape(bsz * seq, d)
        h, route, counts = _router(x2d, norm2_g[l], router_g_w[l], router_g_b[l], router_e_w[l], router_e_b[l],
                                   tok_tile)
        x = _moe(h, route, counts, x2d, exp_w_gate[l], exp_w_up[l], exp_w_down[l], final_g, l == depth - 1,
                 tok_tile).reshape(bsz, seq, d)
    return x
```

```python
import functools
import math

import numpy as np
import jax
import jax.numpy as jnp
from jax import lax
from jax.experimental import pallas as pl
from jax.experimental.pallas import tpu as pltpu

F32 = jnp.float32
BF16 = jnp.bfloat16

SSM_GROUP = 16
SSM_STATE = 64
HEAD_DIM = 64
N_KV_HEADS = 2
GQA = 4
N_HEADS = N_KV_HEADS * GQA
CMP_BLOCK = 32
CMP_STRIDE = 16
SEL_BLOCK = 64
SEL_TOPK = 16
N_LOCAL_BLOCKS = 2
WINDOW = 512
N_BUCKETS = 32
MAX_DISTANCE = 128
N_GROUPS = 4
EXPERTS_PER_GROUP = 8
N_EXPERTS = N_GROUPS * EXPERTS_PER_GROUP
EPS = 1e-6
NEG = -1e30
BIG = 1e9

LANES = 128
SUBLANES = 8
ATT_TILE = 128
V_ROWS = HEAD_DIM + 16
SSM_OCT = 8
VMEM_LIMIT = 56 * 1024 * 1024

HIGHEST = lax.Precision.HIGHEST


def _cparams(*sem):
    return pltpu.CompilerParams(dimension_semantics=sem, vmem_limit_bytes=VMEM_LIMIT)


def _dot(a, b, precision=None):
    return jnp.dot(a, b, preferred_element_type=F32, precision=precision)


def _dot_nt(a, b, precision=None):
    return lax.dot_general(a, b, (((1,), (1,)), ((), ())),
                           preferred_element_type=F32, precision=precision)


def _gelu_tanh(x):
    return 0.5 * x * (1.0 + jnp.tanh(math.sqrt(2.0 / math.pi) * (x + 0.044715 * (x * x * x))))


def _sigmoid(x):
    return 1.0 / (1.0 + jnp.exp(-x))


def _t5_bucket_np(dist):
    n = np.maximum(dist, 0)
    max_exact = N_BUCKETS // 2
    nf = np.maximum(n, 1).astype(np.float32)
    large = max_exact + (np.log(nf / np.float32(max_exact))
                         / np.float32(math.log(MAX_DISTANCE / max_exact))
                         * np.float32(N_BUCKETS - max_exact)).astype(np.int32)
    large = np.minimum(large, N_BUCKETS - 1)
    return np.where(n < max_exact, n, large).astype(np.int32)


def _inproj_kernel(x_ref, g_ref, wu_ref, wc_ref, wk_ref, wqt_ref, wvt_ref, wgt_ref,
                   u_ref, kc_ref, vc_ref, ks_ref, kw_ref, qt_ref, vst_ref, vwt_ref, gt_ref, *, tm):
    x = x_ref[0]
    ms = jnp.mean(x * x, axis=-1, keepdims=True)
    h = (x * lax.rsqrt(ms + EPS) * g_ref[...]).astype(BF16)
    u_ref[...] = _dot(h, wu_ref[...])
    dkv = N_KV_HEADS * HEAD_DIM
    c = _dot(h, wc_ref[...])
    kc_ref[0] = c[:, :dkv]
    vc_ref[0] = c[:, dkv:]
    ka = _dot(h, wk_ref[...])
    blk = (pl.program_id(1) * tm + lax.broadcasted_iota(jnp.int32, (tm, LANES), 0)) // SEL_BLOCK
    onehot = jnp.where(lax.broadcasted_iota(jnp.int32, (tm, LANES), 1) - HEAD_DIM == blk, 1.0, 0.0)
    for hk in range(N_KV_HEADS):
        ks_ref[0, hk] = (ka[:, hk * LANES:(hk + 1) * LANES] + onehot).astype(BF16)
        kw_ref[0, hk] = ka[:, (N_KV_HEADS + hk) * LANES:(N_KV_HEADS + hk + 1) * LANES].astype(BF16)
    qt_ref[0] = (_dot_nt(wqt_ref[...], h) * (HEAD_DIM ** -0.5)).astype(BF16)
    vt = _dot_nt(wvt_ref[...], h).astype(BF16)
    ones_rows = jnp.where(lax.broadcasted_iota(jnp.int32, (V_ROWS - HEAD_DIM, tm), 0) == 0, 1.0, 0.0).astype(BF16)
    for hk in range(N_KV_HEADS):
        vst_ref[0, hk] = jnp.concatenate([vt[hk * HEAD_DIM:(hk + 1) * HEAD_DIM], ones_rows], axis=0)
        vwt_ref[0, hk] = jnp.concatenate([vt[dkv + hk * HEAD_DIM:dkv + (hk + 1) * HEAD_DIM], ones_rows], axis=0)
    gt_ref[0] = _dot_nt(wgt_ref[...], h)


def _inproj(x, g, w_in, tm):
    bsz, seq, d = x.shape
    d_ssm = d // 2
    d_attn = d - d_ssm
    dkv = N_KV_HEADS * HEAD_DIM
    o_kv = d_ssm + d_attn
    o_g = o_kv + 6 * dkv
    assert seq // SEL_BLOCK <= LANES - HEAD_DIM
    wb = w_in.astype(BF16)
    wu = wb[:, :d_ssm]
    wq = wb[:, d_ssm:o_kv]
    k_cmp, v_cmp, k_sel, v_sel, k_win, v_win = [wb[:, o_kv + i * dkv:o_kv + (i + 1) * dkv] for i in range(6)]
    wc = jnp.concatenate([k_cmp, v_cmp], axis=1)

    def pad_heads(w):
        w = w.reshape(d, N_KV_HEADS, HEAD_DIM)
        return jnp.pad(w, ((0, 0), (0, 0), (0, LANES - HEAD_DIM))).reshape(d, N_KV_HEADS * LANES)

    wk = jnp.concatenate([pad_heads(k_sel), pad_heads(k_win)], axis=1)
    wqt = wq.T
    wvt = jnp.concatenate([v_sel, v_win], axis=1).T
    n_gate_rows = 4 * SUBLANES
    wgt = jnp.pad(wb[:, o_g:], ((0, 0), (0, n_gate_rows - 3 * N_HEADS))).T
    full = lambda shape: pl.BlockSpec(shape, lambda b, i: (0,) * len(shape))
    k_spec = pl.BlockSpec((1, N_KV_HEADS, tm, LANES), lambda b, i: (b, 0, i, 0))
    vt_spec = pl.BlockSpec((1, N_KV_HEADS, V_ROWS, tm), lambda b, i: (b, 0, 0, i))
    k_shape = jax.ShapeDtypeStruct((bsz, N_KV_HEADS, seq, LANES), BF16)
    vt_shape = jax.ShapeDtypeStruct((bsz, N_KV_HEADS, V_ROWS, seq), BF16)
    return pl.pallas_call(
        functools.partial(_inproj_kernel, tm=tm),
        grid=(bsz, seq // tm),
        in_specs=[pl.BlockSpec((1, tm, d), lambda b, i: (b, i, 0)),
                  full((1, d)), full(wu.shape), full(wc.shape), full(wk.shape),
                  full(wqt.shape), full(wvt.shape), full(wgt.shape)],
        out_specs=[pl.BlockSpec((tm, d_ssm), lambda b, i: (i, b)),
                   pl.BlockSpec((1, tm, dkv), lambda b, i: (b, i, 0)),
                   pl.BlockSpec((1, tm, dkv), lambda b, i: (b, i, 0)),
                   k_spec, k_spec,
                   pl.BlockSpec((1, d_attn, tm), lambda b, i: (b, 0, i)),
                   vt_spec, vt_spec,
                   pl.BlockSpec((1, n_gate_rows, tm), lambda b, i: (b, 0, i))],
        out_shape=[jax.ShapeDtypeStruct((seq, bsz * d_ssm), F32),
                   jax.ShapeDtypeStruct((bsz, seq, dkv), F32),
                   jax.ShapeDtypeStruct((bsz, seq, dkv), F32),
                   k_shape, k_shape,
                   jax.ShapeDtypeStruct((bsz, d_attn, seq), BF16),
                   vt_shape, vt_shape,
                   jax.ShapeDtypeStruct((bsz, n_gate_rows, seq), F32)],
        compiler_params=_cparams("parallel", "parallel"),
        name="inproj",
    )(x, g.reshape(1, d), wu, wc, wk, wqt, wvt, wgt)


def _s5_kernel(u_ref, bmat_ref, cmat_ref, lam_ref, d_ref, wglu_ref, y_ref, bu_ref, st_ref,
               *, tc, bsz, n_oct, oct_c, oct_s, lane_chunk):
    rows = tc * bsz
    d_ssm = n_oct * oct_c
    n_state = n_oct * oct_s

    @pl.when(pl.program_id(0) == 0)
    def _():
        st_ref[...] = jnp.zeros_like(st_ref)

    u = u_ref[...].reshape(rows, d_ssm)
    ub = u.astype(BF16)
    for o in range(n_oct):
        r = _dot(ub[:, o * oct_c:(o + 1) * oct_c], bmat_ref[o])
        bu_ref[:, o * oct_s:(o + 1) * oct_s] = r[:, :oct_s]
        bu_ref[:, n_state + o * oct_s:n_state + (o + 1) * oct_s] = r[:, oct_s:]

    for c0 in range(0, n_state, lane_chunk):
        re = slice(c0, c0 + lane_chunk)
        im = slice(n_state + c0, n_state + c0 + lane_chunk)

        def step(t, carry, re=re, im=im):
            sr, si = carry
            row = pl.multiple_of(t * bsz, bsz)
            lr = lam_ref[:, re]
            li = lam_ref[:, im]
            nr = lr * sr - li * si + bu_ref[pl.ds(row, bsz), re]
            ni = lr * si + li * sr + bu_ref[pl.ds(row, bsz), im]
            bu_ref[pl.ds(row, bsz), re] = nr
            bu_ref[pl.ds(row, bsz), im] = ni
            return nr, ni

        sr, si = lax.fori_loop(0, tc, step, (st_ref[:, re], st_ref[:, im]))
        st_ref[:, re] = sr
        st_ref[:, im] = si

    ys = []
    for o in range(n_oct):
        s_re = bu_ref[:, o * oct_s:(o + 1) * oct_s].astype(BF16)
        s_im = bu_ref[:, n_state + o * oct_s:n_state + (o + 1) * oct_s].astype(BF16)
        ys.append(_dot(s_re, cmat_ref[o, :oct_s]) + _dot(s_im, cmat_ref[o, oct_s:]))
    y = jnp.concatenate(ys, axis=1) + d_ref[...] * u
    y = _gelu_tanh(y).astype(BF16)
    z = _dot(y, wglu_ref[...])
    out = z[:, :d_ssm] * _sigmoid(z[:, d_ssm:])
    y_ref[...] = out.astype(BF16).reshape(tc, bsz, d_ssm)


def _s5_params(a_re, a_im, b_re, b_im, c_re, c_im, log_dt, bsz):
    g, p = a_re.shape
    h = b_re.shape[-1]
    n_oct = g // SSM_OCT
    dt = jnp.exp(log_dt)[:, None]
    mag = jnp.exp(a_re * dt)
    lr = mag * jnp.cos(a_im * dt)
    li = mag * jnp.sin(a_im * dt)
    den = a_re * a_re + a_im * a_im
    kr = ((lr - 1.0) * a_re + li * a_im) / den
    ki = (li * a_re - (lr - 1.0) * a_im) / den
    bb_re = kr[..., None] * b_re - ki[..., None] * b_im
    bb_im = kr[..., None] * b_im + ki[..., None] * b_re
    eye = jnp.eye(SSM_OCT, dtype=F32)

    def bd_in(m):
        m = m.reshape(n_oct, SSM_OCT, p, h)
        return jnp.einsum('ogph,gk->oghkp', m, eye).reshape(n_oct, SSM_OCT * h, SSM_OCT * p)

    def bd_out(m):
        m = m.reshape(n_oct, SSM_OCT, h, p)
        return jnp.einsum('oghp,gk->ogpkh', m, eye).reshape(n_oct, SSM_OCT * p, SSM_OCT * h)

    bmat = jnp.concatenate([bd_in(bb_re), bd_in(bb_im)], axis=-1).astype(BF16)
    cmat = jnp.concatenate([bd_out(c_re), bd_out(-c_im)], axis=1).astype(BF16)
    lam = jnp.concatenate([lr.reshape(1, g * p), li.reshape(1, g * p)], axis=1)
    lam = jnp.broadcast_to(lam, (bsz, 2 * g * p))
    return bmat, cmat, lam


def _s5(u_tm, bsz, a_re, a_im, b_re, b_im, c_re, c_im, d, log_dt, w_glu, tc):
    seq = u_tm.shape[0]
    d_ssm = u_tm.shape[1] // bsz
    g, p = a_re.shape
    n_oct = g // SSM_OCT
    oct_c = SSM_OCT * SSM_GROUP
    oct_s = SSM_OCT * p
    bmat, cmat, lam = _s5_params(a_re, a_im, b_re, b_im, c_re, c_im, log_dt, bsz)
    u3 = u_tm.reshape(seq, bsz, d_ssm)
    full = lambda shape: pl.BlockSpec(shape, lambda i: (0,) * len(shape))
    kern = functools.partial(_s5_kernel, tc=tc, bsz=bsz, n_oct=n_oct, oct_c=oct_c, oct_s=oct_s,
                             lane_chunk=min(1024, g * p))
    y = pl.pallas_call(
        kern,
        grid=(seq // tc,),
        in_specs=[pl.BlockSpec((tc, bsz, d_ssm), lambda i: (i, 0, 0)),
                  full(bmat.shape), full(cmat.shape), full(lam.shape),
                  full((1, d_ssm)), full(w_glu.shape)],
        out_specs=pl.BlockSpec((tc, bsz, d_ssm), lambda i: (i, 0, 0)),
        out_shape=jax.ShapeDtypeStruct((seq, bsz, d_ssm), BF16),
        scratch_shapes=[pltpu.VMEM((tc * bsz, 2 * g * p), F32),
                        pltpu.VMEM((bsz, 2 * g * p), F32)],
        compiler_params=_cparams("arbitrary"),
        name="s5_mixer",
    )(u3, bmat, cmat, lam, d.reshape(1, d_ssm), w_glu.astype(BF16))
    return y.reshape(seq, bsz * d_ssm)


def _compress_kernel(k_ref, v_ref, w1a_ref, w1b_ref, posw_ref, w2_ref, w2t_ref, kc_ref, vct_ref):
    def hidden(i, src):
        x = src[0]
        a = _dot(x, w1a_ref[i], HIGHEST)
        b = _dot(x, w1b_ref[i], HIGHEST)
        return _gelu_tanh(a + pltpu.roll(b, a.shape[0] - 1, 0) + posw_ref[i])

    kc = _dot(hidden(0, k_ref), w2_ref[...], HIGHEST).astype(BF16)
    vct = _dot_nt(w2t_ref[...], hidden(1, v_ref), HIGHEST).astype(BF16)
    for hk in range(N_KV_HEADS):
        kc_ref[0, hk] = kc[:, hk * HEAD_DIM:(hk + 1) * HEAD_DIM]
        vct_ref[0, hk] = vct[hk * HEAD_DIM:(hk + 1) * HEAD_DIM]


def _compress(k_cmp, v_cmp, cmp_pos, cmp_w1, cmp_w2):
    bsz, seq, dkv = k_cmp.shape
    n_chunk = seq // CMP_STRIDE
    kview = k_cmp.reshape(bsz, n_chunk, CMP_STRIDE * dkv)
    vview = v_cmp.reshape(bsz, n_chunk, CMP_STRIDE * dkv)
    eye = jnp.eye(N_KV_HEADS, dtype=F32)
    w1 = jnp.einsum('ilde,hk->ilhdke', cmp_w1, eye)
    w1 = w1.reshape(2, CMP_BLOCK, dkv, dkv)
    w1a = w1[:, :CMP_STRIDE].reshape(2, CMP_STRIDE * dkv, dkv)
    w1b = w1[:, CMP_STRIDE:].reshape(2, CMP_STRIDE * dkv, dkv)
    posw = jnp.einsum('ild,ilde->ie', cmp_pos, cmp_w1, precision=HIGHEST)
    posw = jnp.tile(posw, (1, N_KV_HEADS)).reshape(2, 1, dkv)
    w2 = jnp.einsum('ief,hk->ihekf', cmp_w2, eye).reshape(2, dkv, dkv)
    full = lambda shape: pl.BlockSpec(shape, lambda b: (0,) * len(shape))
    return pl.pallas_call(
        _compress_kernel,
        grid=(bsz,),
        in_specs=[pl.BlockSpec((1, n_chunk, CMP_STRIDE * dkv), lambda b: (b, 0, 0)),
                  pl.BlockSpec((1, n_chunk, CMP_STRIDE * dkv), lambda b: (b, 0, 0)),
                  full(w1a.shape), full(w1b.shape), full(posw.shape), full((dkv, dkv)), full((dkv, dkv))],
        out_specs=[pl.BlockSpec((1, N_KV_HEADS, n_chunk, HEAD_DIM), lambda b: (b, 0, 0, 0)),
                   pl.BlockSpec((1, N_KV_HEADS, HEAD_DIM, n_chunk), lambda b: (b, 0, 0, 0))],
        out_shape=[jax.ShapeDtypeStruct((bsz, N_KV_HEADS, n_chunk, HEAD_DIM), BF16),
                   jax.ShapeDtypeStruct((bsz, N_KV_HEADS, HEAD_DIM, n_chunk), BF16)],
        compiler_params=_cparams("parallel"),
        name="compress",
    )(kview, vview, w1a, w1b, posw, w2[0], w2[1].T)


CMP_NEAR_BACK = 2 * SUBLANES
CMP_NEAR_ROWS = 3 * SUBLANES


def _bias_tables(rel_bias, seq):
    last = rel_bias[N_BUCKETS - 1]
    assert int(_t5_bucket_np(np.arange(ATT_TILE + 1, max(seq, 2 * ATT_TILE))).min()) == N_BUCKETS - 1
    s = np.arange(ATT_TILE)[:, None]
    t = np.arange(ATT_TILE)[None, :]

    def table(dist):
        return jnp.transpose(rel_bias[jnp.asarray(_t5_bucket_np(dist))] - last, (2, 0, 1))

    causal = jnp.asarray(t >= s)
    pd = jnp.where(causal, table(t - s), NEG)
    p1 = table(ATT_TILE + t - s)
    w_old = jnp.where(jnp.asarray(s > t), 0.0, NEG).astype(F32)
    m = np.arange(CMP_NEAR_ROWS)[:, None] - CMP_NEAR_BACK
    dist_c = t - (CMP_STRIDE * m + CMP_BLOCK - 1)
    assert dist_c[0].min() > ATT_TILE
    pc = jnp.where(jnp.asarray(dist_c >= 0), table(dist_c), 0.0)
    return pd, p1, w_old, pc


def _cmp_attn_kernel(qt_ref, kc_ref, vct_ref, pc_ref, ovl_ref, ocmpt_ref, negm_ref, s_scr,
                     *, tq, n_sel, n_top):
    qi = pl.program_id(1)
    n_chunk = kc_ref.shape[2]
    n_i = lax.broadcasted_iota(jnp.int32, (n_chunk, tq), 0)
    t_i = qi * tq + lax.broadcasted_iota(jnp.int32, (n_chunk, tq), 1)
    valid = n_i * CMP_STRIDE + (CMP_BLOCK - 1) <= t_i
    jrow = lax.broadcasted_iota(jnp.int32, (n_sel, tq), 0)
    blk_t = (qi * tq + lax.broadcasted_iota(jnp.int32, (n_sel, tq), 1)) // SEL_BLOCK
    forced = (jrow == 0) | ((jrow <= blk_t) & (jrow > blk_t - N_LOCAL_BLOCKS))
    future = jrow > blk_t
    jl = lax.broadcasted_iota(jnp.int32, (SUBLANES, tq), 0)
    for hk in range(N_KV_HEADS):
        s_scr[hk, 0:CMP_NEAR_BACK, :] = jnp.zeros((CMP_NEAR_BACK, GQA * tq), F32)
    near = pl.ds(pl.multiple_of(qi * (tq // CMP_STRIDE), SUBLANES), CMP_NEAR_ROWS)
    body = slice(CMP_NEAR_BACK, CMP_NEAR_BACK + n_chunk)
    for hk in range(N_KV_HEADS):
        kc = kc_ref[0, hk]
        vct = vct_ref[0, hk]
        heads = range(hk * GQA, (hk + 1) * GQA)
        q4 = jnp.concatenate([qt_ref[0, hd * HEAD_DIM:(hd + 1) * HEAD_DIM, :] for hd in heads], axis=1)
        s_scr[hk, body, :] = _dot(kc, q4)
        s_scr[hk, near, :] = s_scr[hk, near, :] + jnp.concatenate([pc_ref[hd] for hd in heads], axis=1)
        valid4 = jnp.concatenate([valid] * GQA, axis=1)
        s = jnp.where(valid4, s_scr[hk, body, :], NEG)
        m = jnp.max(s, axis=0, keepdims=True)
        e = jnp.where(valid4, jnp.exp(s - m), 0.0)
        l = jnp.sum(e, axis=0, keepdims=True)
        p = e * jnp.where(l > 0.0, 1.0 / l, 0.0)
        o4 = _dot(vct, p.astype(BF16))
        psum = jnp.zeros((n_chunk, tq), F32)
        for g, hd in enumerate(heads):
            psum = psum + p[:, g * tq:(g + 1) * tq]
            ocmpt_ref[0, hd * HEAD_DIM:(hd + 1) * HEAD_DIM, :] = o4[:, g * tq:(g + 1) * tq]
        imp = _dot(ovl_ref[...], psum, HIGHEST)
        v = jnp.where(forced, BIG, jnp.where(future, -BIG, imp))
        n_t = n_sel // SUBLANES
        vt = [v[a * SUBLANES:(a + 1) * SUBLANES] for a in range(n_t)]
        rank = [jnp.zeros((SUBLANES, tq), F32) for _ in range(n_t)]
        for jp in range(n_sel):
            row = jnp.broadcast_to(v[jp:jp + 1, :], (SUBLANES, tq))
            for a in range(n_t):
                if a > jp // SUBLANES:
                    hit = jnp.where(row >= vt[a], 1.0, 0.0)
                elif a < jp // SUBLANES:
                    hit = jnp.where(row > vt[a], 1.0, 0.0)
                else:
                    hit = jnp.where(jl > jp % SUBLANES, jnp.where(row >= vt[a], 1.0, 0.0),
                                    jnp.where(row > vt[a], 1.0, 0.0))
                rank[a] = rank[a] + hit
        rank = jnp.concatenate(rank, axis=0)
        negm_ref[0, hk] = jnp.where(rank < float(n_top), 0.0, NEG).astype(BF16)


def _cmp_attn(qt, kc, vct, pc, tq):
    bsz, d_attn, seq = qt.shape
    n_chunk = kc.shape[2]
    n_sel = seq // SEL_BLOCK
    n_top = min(SEL_TOPK, n_sel)
    n_cmp = (seq - CMP_BLOCK) // CMP_STRIDE + 1
    assert n_sel % SUBLANES == 0 and tq % CMP_STRIDE == 0 and (tq // CMP_STRIDE) % SUBLANES == 0
    cs = np.arange(n_chunk)[:, None] * CMP_STRIDE
    ss = np.arange(n_sel)[None, :] * SEL_BLOCK
    ovl = np.clip(np.minimum(cs + CMP_BLOCK, ss + SEL_BLOCK) - np.maximum(cs, ss), 0, None) / CMP_BLOCK
    ovl[n_cmp:] = 0.0
    ovl_t = jnp.asarray(ovl.T, dtype=F32)
    kern = functools.partial(_cmp_attn_kernel, tq=tq, n_sel=n_sel, n_top=n_top)
    return pl.pallas_call(
        kern,
        grid=(bsz, seq // tq),
        in_specs=[pl.BlockSpec((1, d_attn, tq), lambda b, i: (b, 0, i)),
                  pl.BlockSpec((1, N_KV_HEADS, n_chunk, HEAD_DIM), lambda b, i: (b, 0, 0, 0)),
                  pl.BlockSpec((1, N_KV_HEADS, HEAD_DIM, n_chunk), lambda b, i: (b, 0, 0, 0)),
                  pl.BlockSpec(pc.shape, lambda b, i: (0, 0, 0)),
                  pl.BlockSpec((n_sel, n_chunk), lambda b, i: (0, 0))],
        out_specs=[pl.BlockSpec((1, d_attn, tq), lambda b, i: (b, 0, i)),
                   pl.BlockSpec((1, N_KV_HEADS, n_sel, tq), lambda b, i: (b, 0, 0, i))],
        out_shape=[jax.ShapeDtypeStruct((bsz, d_attn, seq), F32),
                   jax.ShapeDtypeStruct((bsz, N_KV_HEADS, n_sel, seq), BF16)],
        scratch_shapes=[pltpu.VMEM((N_KV_HEADS, CMP_NEAR_BACK + n_chunk, GQA * tq), F32)],
        compiler_params=_cparams("parallel", "parallel"),
        name="cmp_attn",
    )(qt, kc, vct, pc, ovl_t)


def _sw_attn_kernel(qt_ref, negm_ref, ks_ref, vst_ref, kw_ref, vwt_ref, ocmpt_ref, gt_ref,
                    pd_ref, p1_ref, wold_ref, eye_ref, o_ref, qa_ref, s_buf, acc_s, acc_w, *, tq, n_win_tiles):
    qi = pl.program_id(1)
    gates = _sigmoid(gt_ref[0])

    def tile(kt):
        return pl.ds(pl.multiple_of(kt * ATT_TILE, ATT_TILE), ATT_TILE)

    def pair(j):
        return pl.ds(pl.multiple_of(j * (2 * ATT_TILE), 2 * ATT_TILE), 2 * ATT_TILE)

    def col_max(s):
        return jnp.max(s, axis=0, keepdims=True)

    def exists(back):
        return jnp.where(qi >= back, 0.0, NEG)

    def near_bias(ref, hk):
        return jnp.concatenate([ref[hd] for hd in range(hk * GQA, (hk + 1) * GQA)], axis=1)

    def softmax_first(s_list, v_list, acc, hk):
        m = functools.reduce(jnp.maximum, [col_max(s) for s in s_list])
        acc[hk] = sum(_dot(v, jnp.exp(s - m).astype(BF16)) for v, s in zip(v_list, s_list))
        return m

    w_old = jnp.concatenate([wold_ref[...]] * GQA, axis=1)
    n_far = jnp.maximum(qi - 1, 0)
    n_pairs = n_far // 2
    k_prev = jnp.maximum(qi - 1, 0)
    k_odd = jnp.maximum(n_far - 1, 0)
    odd = jnp.where(n_far % 2 == 1, 0.0, NEG)

    m_sel = []
    c_max = []
    for hk in range(N_KV_HEADS):
        heads = range(hk * GQA, (hk + 1) * GQA)
        q4 = jnp.concatenate([qt_ref[0, hd * HEAD_DIM:(hd + 1) * HEAD_DIM, :] for hd in heads], axis=1)
        nm = negm_ref[0, hk]
        if nm.shape[0] < LANES - HEAD_DIM:
            nm = jnp.concatenate([nm, jnp.zeros((LANES - HEAD_DIM - nm.shape[0], tq), BF16)], axis=0)
        qa = jnp.concatenate([q4, jnp.concatenate([nm] * GQA, axis=1)], axis=0)
        qa_ref[hk] = qa
        qw = jnp.concatenate([q4, jnp.zeros_like(q4)], axis=0)
        pd4 = near_bias(pd_ref, hk)
        p14 = near_bias(p1_ref, hk)

        s_near = [_dot(ks_ref[0, hk, tile(qi), :], qa) + pd4,
                  _dot(ks_ref[0, hk, tile(k_prev), :], qa) + (p14 + exists(1)),
                  _dot(ks_ref[0, hk, tile(k_odd), :], qa) + odd]
        v_near = [vst_ref[0, hk, :, tile(qi)], vst_ref[0, hk, :, tile(k_prev)], vst_ref[0, hk, :, tile(k_odd)]]
        m_sel.append(softmax_first(s_near, v_near, acc_s, hk))

        s_win = [_dot(kw_ref[0, hk, tile(qi), :], qw) + pd4]
        v_win = [vwt_ref[0, hk, :, tile(qi)]]
        for back in range(1, n_win_tiles + 1):
            kt = jnp.maximum(qi - back, 0)
            s = _dot(kw_ref[0, hk, tile(kt), :], qw)
            if back == 1:
                s = s + (p14 + exists(back))
            elif back == n_win_tiles:
                s = s + (w_old + exists(back))
            else:
                s = s + exists(back)
            s_win.append(s)
            v_win.append(vwt_ref[0, hk, :, tile(kt)])
        softmax_first(s_win, v_win, acc_w, hk)

        s0 = _dot(ks_ref[0, hk, pair(0), :], qa)
        s_buf[hk] = s0
        c_max.append(col_max(s0))

    def far(j, carry):
        ms, cm = carry[:N_KV_HEADS], carry[N_KV_HEADS:]
        s_next = [_dot(ks_ref[0, hk, pair(j + 1), :], qa_ref[hk]) for hk in range(N_KV_HEADS)]
        m_new = []
        for hk in range(N_KV_HEADS):
            m = jnp.maximum(ms[hk], cm[hk])
            p = jnp.exp(s_buf[hk] - m).astype(BF16)
            acc_s[hk] = jnp.exp(ms[hk] - m) * acc_s[hk] + _dot(vst_ref[0, hk, :, pair(j)], p)
            m_new.append(m)
        c_new = []
        for hk in range(N_KV_HEADS):
            c_new.append(col_max(s_next[hk]))
            s_buf[hk] = s_next[hk]
        return (*m_new, *c_new)

    lax.fori_loop(0, n_pairs, far, (*m_sel, *c_max))

    ys = []
    for hk in range(N_KV_HEADS):
        o_sel = acc_s[hk, :HEAD_DIM] * (1.0 / acc_s[hk, HEAD_DIM:HEAD_DIM + 1])
        o_win = acc_w[hk, :HEAD_DIM] * (1.0 / acc_w[hk, HEAD_DIM:HEAD_DIM + 1])
        for g in range(GQA):
            hd = hk * GQA + g
            cols = slice(g * tq, (g + 1) * tq)
            ys.append(gates[3 * hd:3 * hd + 1] * ocmpt_ref[0, hd * HEAD_DIM:(hd + 1) * HEAD_DIM, :]
                      + gates[3 * hd + 1:3 * hd + 2] * o_sel[:, cols]
                      + gates[3 * hd + 2:3 * hd + 3] * o_win[:, cols])
    yt = jnp.concatenate(ys, axis=0).astype(BF16)
    o_ref[0] = _dot_nt(eye_ref[...], yt).astype(BF16)


def _sw_attn(qt, negm, ks, vst, kw, vwt, ocmpt, gt, pd, p1, w_old, tq):
    bsz, d_attn, seq = qt.shape
    n_sel = negm.shape[2]
    assert tq == ATT_TILE and WINDOW % ATT_TILE == 0 and n_sel <= LANES - HEAD_DIM and seq >= 2 * ATT_TILE
    eye = jnp.eye(tq, dtype=BF16)
    kern = functools.partial(_sw_attn_kernel, tq=tq, n_win_tiles=WINDOW // ATT_TILE)
    k_spec = pl.BlockSpec((1, N_KV_HEADS, seq, LANES), lambda b, i: (b, 0, 0, 0))
    vt_spec = pl.BlockSpec((1, N_KV_HEADS, V_ROWS, seq), lambda b, i: (b, 0, 0, 0))
    qt_spec = pl.BlockSpec((1, d_attn, tq), lambda b, i: (b, 0, i))
    full = lambda shape: pl.BlockSpec(shape, lambda b, i: (0,) * len(shape))
    return pl.pallas_call(
        kern,
        grid=(bsz, seq // tq),
        in_specs=[qt_spec,
                  pl.BlockSpec((1, N_KV_HEADS, n_sel, tq), lambda b, i: (b, 0, 0, i)),
                  k_spec, vt_spec, k_spec, vt_spec,
                  qt_spec,
                  pl.BlockSpec((1, gt.shape[1], tq), lambda b, i: (b, 0, i)),
                  full(pd.shape), full(p1.shape), full(w_old.shape), full(eye.shape)],
        out_specs=pl.BlockSpec((1, tq, d_attn), lambda b, i: (b, i, 0)),
        out_shape=jax.ShapeDtypeStruct((bsz, seq, d_attn), BF16),
        scratch_shapes=[pltpu.VMEM((N_KV_HEADS, LANES, GQA * tq), BF16),
                        pltpu.VMEM((N_KV_HEADS, 2 * ATT_TILE, GQA * tq), F32),
                        pltpu.VMEM((N_KV_HEADS, V_ROWS, GQA * tq), F32),
                        pltpu.VMEM((N_KV_HEADS, V_ROWS, GQA * tq), F32)],
        compiler_params=_cparams("parallel", "parallel"),
        name="sel_win_attn",
    )(qt, negm, ks, vst, kw, vwt, ocmpt, gt, pd, p1, w_old, eye)


def _outproj_kernel(x_ref, ys_ref, ya_ref, ws_ref, wa_ref, o_ref):
    o_ref[0] = x_ref[0] + _dot(ys_ref[...], ws_ref[...]) + _dot(ya_ref[0], wa_ref[...])


def _outproj(x, y_ssm_tm, y_attn, w_out, tm):
    bsz, seq, d = x.shape
    d_ssm = y_ssm_tm.shape[1] // bsz
    d_attn = y_attn.shape[2]
    wb = w_out.astype(BF16)
    ws, wa = wb[:d_ssm], wb[d_ssm:]
    full = lambda shape: pl.BlockSpec(shape, lambda b, i: (0,) * len(shape))
    return pl.pallas_call(
        _outproj_kernel,
        grid=(bsz, seq // tm),
        in_specs=[pl.BlockSpec((1, tm, d), lambda b, i: (b, i, 0)),
                  pl.BlockSpec((tm, d_ssm), lambda b, i: (i, b)),
                  pl.BlockSpec((1, tm, d_attn), lambda b, i: (b, i, 0)),
                  full(ws.shape), full(wa.shape)],
        out_specs=pl.BlockSpec((1, tm, d), lambda b, i: (b, i, 0)),
        out_shape=jax.ShapeDtypeStruct((bsz, seq, d), F32),
        compiler_params=_cparams("parallel", "parallel"),
        name="outproj",
    )(x, y_ssm_tm, y_attn, ws, wa)


def _router_kernel(x_ref, g_ref, wr_ref, br_ref, tri_ref, h_ref, route_ref, cnt_ref, run_ref):
    @pl.when(pl.program_id(0) == 0)
    def _():
        run_ref[...] = jnp.zeros_like(run_ref)

    x = x_ref[...]
    ms = jnp.mean(x * x, axis=-1, keepdims=True)
    h = x * lax.rsqrt(ms + EPS) * g_ref[...]
    h_ref[...] = h.astype(BF16)
    logits = _dot(h, wr_ref[...], HIGHEST) + br_ref[...]
    lane = lax.broadcasted_iota(jnp.int32, logits.shape, 1)
    is_grp = (lane >= N_EXPERTS) & (lane < N_EXPERTS + N_GROUPS)
    lg = jnp.where(is_grp, logits, NEG)
    mg = jnp.max(lg, axis=-1, keepdims=True)
    g_val = 1.0 / jnp.sum(jnp.where(is_grp, jnp.exp(lg - mg), 0.0), axis=-1, keepdims=True)
    g_idx = jnp.min(jnp.where(is_grp & (lg == mg), lane - N_EXPERTS, N_GROUPS), axis=-1, keepdims=True)
    in_grp = (lane >= g_idx * EXPERTS_PER_GROUP) & (lane < (g_idx + 1) * EXPERTS_PER_GROUP)
    le = jnp.where(in_grp, logits, NEG)
    me = jnp.max(le, axis=-1, keepdims=True)
    ee = jnp.where(in_grp, jnp.exp(le - me), 0.0)
    pe = ee / jnp.sum(ee, axis=-1, keepdims=True)
    p1 = jnp.max(pe, axis=-1, keepdims=True)
    i1 = jnp.min(jnp.where(in_grp & (pe == p1), lane, LANES), axis=-1, keepdims=True)
    rest = in_grp & (lane != i1)
    pr = jnp.where(rest, pe, -1.0)
    p2 = jnp.max(pr, axis=-1, keepdims=True)
    i2 = jnp.min(jnp.where(rest & (pr == p2), lane, LANES), axis=-1, keepdims=True)
    tot = p1 + p2
    oh1 = jnp.where(lane == i1, 1.0, 0.0)
    oh2 = jnp.where(lane == i2, 1.0, 0.0)
    both = oh1 + oh2
    before = _dot(tri_ref[...], both.astype(BF16)) + run_ref[...]
    r1 = jnp.sum(oh1 * before, axis=-1, keepdims=True)
    r2 = jnp.sum(oh2 * before, axis=-1, keepdims=True)
    run_ref[...] = run_ref[...] + jnp.sum(both, axis=0, keepdims=True)
    cnt_ref[...] = run_ref[...]
    vals = (i1.astype(F32), i2.astype(F32), g_val * (p1 / tot), g_val * (p2 / tot), r1, r2)
    out = jnp.zeros(logits.shape, F32)
    for k, val in enumerate(vals):
        out = jnp.where(lane == k, val, out)
    route_ref[...] = out


def _router(x2d, g, rg_w, rg_b, re_w, re_b, tm):
    tok, d = x2d.shape
    we = jnp.transpose(re_w, (1, 0, 2)).reshape(d, N_EXPERTS)
    wr = jnp.pad(jnp.concatenate([we, rg_w], axis=1), ((0, 0), (0, LANES - N_EXPERTS - N_GROUPS)))
    br = jnp.pad(jnp.concatenate([re_b.reshape(-1), rg_b]), (0, LANES - N_EXPERTS - N_GROUPS)).reshape(1, LANES)
    tri = jnp.asarray(np.tril(np.ones((tm, tm), np.float32), -1), dtype=BF16)
    full = lambda shape: pl.BlockSpec(shape, lambda i: (0,) * len(shape))
    return pl.pallas_call(
        _router_kernel,
        grid=(tok // tm,),
        in_specs=[pl.BlockSpec((tm, d), lambda i: (i, 0)), full((1, d)), full(wr.shape), full(br.shape),
                  full(tri.shape)],
        out_specs=[pl.BlockSpec((tm, d), lambda i: (i, 0)), pl.BlockSpec((tm, LANES), lambda i: (i, 0)),
                   full((1, LANES))],
        out_shape=[jax.ShapeDtypeStruct((tok, d), BF16), jax.ShapeDtypeStruct((tok, LANES), F32),
                   jax.ShapeDtypeStruct((1, LANES), F32)],
        scratch_shapes=[pltpu.VMEM((1, LANES), F32)],
        compiler_params=_cparams("arbitrary"),
        name="router",
    )(x2d, g.reshape(1, d), wr, br, tri)


def _dest_kernel(route_ref, start_ref, o_ref):
    r = route_ref[...]
    lane = lax.broadcasted_iota(jnp.int32, r.shape, 1)
    start = start_ref[...]
    d1 = r[:, 4:5] + jnp.sum(jnp.where(lane == r[:, 0:1].astype(jnp.int32), start, 0.0), axis=-1, keepdims=True)
    d2 = r[:, 5:6] + jnp.sum(jnp.where(lane == r[:, 1:2].astype(jnp.int32), start, 0.0), axis=-1, keepdims=True)
    o_ref[...] = jnp.where(lane == 0, d1, jnp.where(lane == 1, d2, 0.0)).astype(jnp.int32)


def _dest(route, cstart, tm):
    tok = route.shape[0]
    start = jnp.pad(cstart.astype(F32), (0, LANES - N_EXPERTS)).reshape(1, LANES)
    return pl.pallas_call(
        _dest_kernel,
        grid=(tok // tm,),
        in_specs=[pl.BlockSpec((tm, LANES), lambda i: (i, 0)), pl.BlockSpec((1, LANES), lambda i: (0, 0))],
        out_specs=pl.BlockSpec((tm, LANES), lambda i: (i, 0)),
        out_shape=jax.ShapeDtypeStruct((tok, LANES), jnp.int32),
        compiler_params=_cparams("parallel"),
        name="moe_dest",
    )(route, start)


MOE_ROW_TILE = 256


def _visit_plan(counts, n_pairs):
    cend = jnp.cumsum(counts)
    cstart = cend - counts
    first_tile = cstart // MOE_ROW_TILE
    last_tile = (cend - 1) // MOE_ROW_TILE
    n_vis = jnp.where(counts > 0, last_tile - first_tile + 1, 0)
    vis_end = jnp.cumsum(n_vis)
    n_visits = vis_end[-1]
    max_visits = n_pairs // MOE_ROW_TILE + N_EXPERTS - 1
    v = jnp.minimum(jnp.arange(max_visits, dtype=jnp.int32), n_visits - 1)
    ve = jnp.searchsorted(vis_end, v, side='right').astype(jnp.int32)
    vt = first_tile[ve] + (v - (vis_end[ve] - n_vis[ve]))
    vfirst = jnp.concatenate([jnp.ones((1,), jnp.int32), (vt[1:] != vt[:-1]).astype(jnp.int32)])
    meta = jnp.stack([ve, vt.astype(jnp.int32), vfirst, cstart[ve], cend[ve]]).astype(jnp.int32)
    return cstart, meta, n_visits.reshape(1).astype(jnp.int32)


def _moe_kernel(meta_ref, nv_ref, x_ref, w_ref, wg_ref, wu_ref, wd_ref, o_ref):
    v = pl.program_id(0)

    @pl.when(v < nv_ref[0])
    def _():
        row = meta_ref[1, v] * MOE_ROW_TILE + lax.broadcasted_iota(jnp.int32, (MOE_ROW_TILE, 1), 0)
        mine = (row >= meta_ref[3, v]) & (row < meta_ref[4, v])
        x = x_ref[...]
        a = _dot(x, wg_ref[0].astype(BF16))
        hid = jnp.where(mine, (a * _sigmoid(a)) * _dot(x, wu_ref[0].astype(BF16)) * w_ref[...], 0.0)
        y = _dot(hid.astype(BF16), wd_ref[0].astype(BF16))

        @pl.when(meta_ref[2, v] == 1)
        def _():
            o_ref[...] = y

        @pl.when(meta_ref[2, v] == 0)
        def _():
            o_ref[...] += y


def _moe_experts(x_sorted, w_sorted, meta, n_visits, w_gate, w_up, w_down):
    rows, d = x_sorted.shape
    f = w_gate.shape[-1]
    grid_spec = pltpu.PrefetchScalarGridSpec(
        num_scalar_prefetch=2,
        grid=(meta.shape[1],),
        in_specs=[pl.BlockSpec((MOE_ROW_TILE, d), lambda v, m, n: (m[1, v], 0)),
                  pl.BlockSpec((MOE_ROW_TILE, 1), lambda v, m, n: (m[1, v], 0)),
                  pl.BlockSpec((1, d, f), lambda v, m, n: (m[0, v], 0, 0)),
                  pl.BlockSpec((1, d, f), lambda v, m, n: (m[0, v], 0, 0)),
                  pl.BlockSpec((1, f, d), lambda v, m, n: (m[0, v], 0, 0))],
        out_specs=pl.BlockSpec((MOE_ROW_TILE, d), lambda v, m, n: (m[1, v], 0)),
    )
    return pl.pallas_call(
        _moe_kernel,
        grid_spec=grid_spec,
        out_shape=jax.ShapeDtypeStruct((rows, d), F32),
        compiler_params=_cparams("arbitrary"),
        name="moe_experts",
    )(meta, n_visits, x_sorted, w_sorted.reshape(rows, 1), w_gate.reshape(N_EXPERTS, d, f),
      w_up.reshape(N_EXPERTS, d, f), w_down.reshape(N_EXPERTS, f, d))


def _combine_kernel(x_ref, y1_ref, y2_ref, g_ref, o_ref, *, final_norm):
    x = x_ref[...] + y1_ref[...] + y2_ref[...]
    if final_norm:
        ms = jnp.mean(x * x, axis=-1, keepdims=True)
        x = x * lax.rsqrt(ms + EPS) * g_ref[...]
    o_ref[...] = x


def _combine(x2d, y1, y2, g, final_norm, tm):
    tok, d = x2d.shape
    blk = pl.BlockSpec((tm, d), lambda i: (i, 0))
    return pl.pallas_call(
        functools.partial(_combine_kernel, final_norm=final_norm),
        grid=(tok // tm,),
        in_specs=[blk, blk, blk, pl.BlockSpec((1, d), lambda i: (0, 0))],
        out_specs=blk,
        out_shape=jax.ShapeDtypeStruct((tok, d), F32),
        compiler_params=_cparams("parallel"),
        name="moe_combine",
    )(x2d, y1, y2, g.reshape(1, d))


def _moe(h, route, counts, x2d, w_gate, w_up, w_down, g_final, final_norm, tm):
    tok = x2d.shape[0]
    cstart, meta, n_visits = _visit_plan(counts[0, :N_EXPERTS].astype(jnp.int32), 2 * tok)
    dest = _dest(route, cstart, tm)
    d1, d2 = dest[:, 0], dest[:, 1]
    ptok = jnp.arange(tok, dtype=jnp.int32)
    _, tok_sorted, w_sorted = lax.sort((jnp.concatenate([d1, d2]), jnp.concatenate([ptok, ptok]),
                                        jnp.concatenate([route[:, 2], route[:, 3]])), num_keys=1)
    x_sorted = jnp.take(h, tok_sorted, axis=0, mode="clip")
    y_sorted = _moe_experts(x_sorted, w_sorted, meta, n_visits, w_gate, w_up, w_down)
    y1 = jnp.take(y_sorted, d1, axis=0, mode="clip")
    y2 = jnp.take(y_sorted, d2, axis=0, mode="clip")
    return _combine(x2d, y1, y2, g_final, final_norm, tm)


def kernel(x, norm1_g, norm2_g, final_g, w_in, w_out, ssm_a_re, ssm_a_im, ssm_b_re, ssm_b_im, ssm_c_re, ssm_c_im, ssm_d, ssm_log_dt, ssm_w_glu, cmp_pos, cmp_w1, cmp_w2, rel_bias, router_g_w, router_g_b, router_e_w, router_e_b, exp_w_gate, exp_w_up, exp_w_down):
    bsz, seq, d = x.shape
    depth = norm1_g.shape[0]
    tm = min(512, seq)
    tok_tile = min(1024, bsz * seq)
    pd, p1, w_old, pc = _bias_tables(rel_bias, seq)
    for l in range(depth):
        u_tm, k_cmp, v_cmp, ks, kw, qt, vst, vwt, gt = _inproj(x, norm1_g[l], w_in[l], tm)
        y_ssm = _s5(u_tm, bsz, ssm_a_re[l], ssm_a_im[l], ssm_b_re[l], ssm_b_im[l], ssm_c_re[l],
                    ssm_c_im[l], ssm_d[l].reshape(-1), ssm_log_dt[l], ssm_w_glu[l], tc=min(32, seq))
        kc, vct = _compress(k_cmp, v_cmp, cmp_pos[l], cmp_w1[l], cmp_w2[l])
        ocmpt, negm = _cmp_attn(qt, kc, vct, pc, ATT_TILE)
        y_attn = _sw_attn(qt, negm, ks, vst, kw, vwt, ocmpt, gt, pd, p1, w_old, ATT_TILE)
        x = _outproj(x, y_ssm, y_attn, w_out[l], tm)
        x2d = x.reshape(bsz * seq, d)
        h, route, counts = _router(x2d, norm2_g[l], router_g_w[l], router_g_b[l], router_e_w[l], router_e_b[l],
                                   tok_tile)
        x = _moe(h, route, counts, x2d, exp_w_gate[l], exp_w_up[l], exp_w_down[l], final_g, l == depth - 1,
                 tok_tile).reshape(bsz, seq, d)
    return x
```
